```python
import jax, jax.numpy as jnp
from jax import lax
import numpy as np


D_MODEL = 1024
BATCH = 8
SEQ = 4096
DEPTH = 2

D_FF = 2816
FFN_RES_SCALE = 0.5
RMS_EPS = 1e-6
PLE_DIM = 256
QBLK = 128
A_HEADS = 8
A_KV_HEADS = 2
A_GROUP = A_HEADS // A_KV_HEADS
A_HEAD_DIM = 64
WINDOW = 128
B_HEADS = 8
B_Q_LORA = 256
B_KV_LORA = 128
B_NOPE_DIM = 64
B_ROPE_DIM = 32
B_V_DIM = 64
ROPE_THETA = 10000.0
C_HEADS = 16
C_HEAD_DIM = 64
FORGET_BIAS_CENTER = 3.0
N_EVEN = (DEPTH + 1) // 2
N_ODD = DEPTH // 2
EVEN_IN_SPLITS = (A_HEADS * A_HEAD_DIM, A_KV_HEADS * A_HEAD_DIM, A_KV_HEADS * A_HEAD_DIM, B_Q_LORA, B_KV_LORA, B_ROPE_DIM)
EVEN_IN_DIM = A_HEADS * A_HEAD_DIM + 2 * A_KV_HEADS * A_HEAD_DIM + B_Q_LORA + B_KV_LORA + B_ROPE_DIM
EVEN_MIX_DIM = A_HEADS * A_HEAD_DIM + B_HEADS * B_V_DIM
ODD_MIX_DIM = C_HEADS * C_HEAD_DIM
ODD_IN_DIM = 3 * ODD_MIX_DIM + C_HEADS

kernel_name = 'hybrid_swa_mla_fox_macaron'


def rms_norm(x, g):
    xf = x.astype(jnp.float32)
    y = xf * lax.rsqrt(jnp.mean(xf * xf, axis=-1, keepdims=True) + RMS_EPS)
    return (y * g.astype(jnp.float32)).astype(x.dtype)


def swiglu(x, w_gate_up, w_down):
    g, u = jnp.split(x @ w_gate_up, 2, axis=-1)
    return (jax.nn.silu(g) * u) @ w_down


def alibi_slopes(n):
    return 2.0 ** (-8.0 * jnp.arange(1, n + 1, dtype=jnp.float32) / n)


def rope_tables(seq, dim):
    inv = ROPE_THETA ** (-jnp.arange(0, dim, 2, dtype=jnp.float32) / dim)
    ang = jnp.arange(seq, dtype=jnp.float32)[:, None] * inv[None, :]
    return jnp.cos(ang), jnp.sin(ang)


def apply_rope(x, cos, sin):
    half = x.shape[-1] // 2
    x1 = x[..., :half].astype(jnp.float32)
    x2 = x[..., half:].astype(jnp.float32)
    return jnp.concatenate([x1 * cos - x2 * sin, x1 * sin + x2 * cos], axis=-1).astype(x.dtype)


def swa_sink_attention(q, k, v, sinks):
    B, S = q.shape[0], q.shape[1]
    nb = S // WINDOW
    qb = q.reshape(B, nb, WINDOW, A_KV_HEADS, A_GROUP, A_HEAD_DIM)
    pad = jnp.zeros((B, WINDOW, A_KV_HEADS, A_HEAD_DIM), k.dtype)
    kp = jnp.concatenate([pad, k], axis=1).reshape(B, nb + 1, WINDOW, A_KV_HEADS, A_HEAD_DIM)
    vp = jnp.concatenate([pad, v], axis=1).reshape(B, nb + 1, WINDOW, A_KV_HEADS, A_HEAD_DIM)
    kb = jnp.concatenate([kp[:, :-1], kp[:, 1:]], axis=2)
    vb = jnp.concatenate([vp[:, :-1], vp[:, 1:]], axis=2)
    s = jnp.einsum('bnqkgd,bnskd->bnkgqs', qb, kb).astype(jnp.float32) * (A_HEAD_DIM ** -0.5)
    qi = jnp.arange(WINDOW)[:, None]
    kj = jnp.arange(2 * WINDOW)[None, :]
    dist = qi + WINDOW - kj
    band = (dist >= 0) & (dist < WINDOW)
    start_ok = (jnp.arange(nb)[:, None, None] * WINDOW + kj[None] - WINDOW) >= 0
    mask = band[None] & start_ok
    slopes = alibi_slopes(A_HEADS).reshape(A_KV_HEADS, A_GROUP)
    s = s - slopes[None, None, :, :, None, None] * dist.astype(jnp.float32)[None, None, None, None]
    s = jnp.where(mask[None, :, None, None], s, -jnp.inf)
    sink = sinks.astype(jnp.float32).reshape(A_KV_HEADS, A_GROUP)[None, None, :, :, None, None]
    m = jnp.maximum(jnp.max(s, axis=-1, keepdims=True), sink)
    e = jnp.exp(s - m)
    pr = e / (jnp.sum(e, axis=-1, keepdims=True) + jnp.exp(sink - m))
    out = jnp.einsum('bnkgqs,bnskd->bnqkgd', pr.astype(v.dtype), vb)
    return out.reshape(B, S, A_HEADS * A_HEAD_DIM)


def mla_attention(q_nope, q_rope, k_nope, k_rope, v):
    B, S = q_nope.shape[0], q_nope.shape[1]
    nb = S // QBLK
    qn = q_nope.reshape(B, nb, QBLK, B_HEADS, B_NOPE_DIM).transpose(1, 0, 2, 3, 4)
    qr = q_rope.reshape(B, nb, QBLK, B_HEADS, B_ROPE_DIM).transpose(1, 0, 2, 3, 4)
    kpos = jnp.arange(S)
    scale = (B_NOPE_DIM + B_ROPE_DIM) ** -0.5

    def one_block(args):
        qn_b, qr_b, n = args
        s = jnp.einsum('bqhd,bkhd->bhqk', qn_b, k_nope) + jnp.einsum('bqhd,bkd->bhqk', qr_b, k_rope)
        s = s.astype(jnp.float32) * scale
        qpos = n * QBLK + jnp.arange(QBLK)
        s = jnp.where(kpos[None, :] <= qpos[:, None], s, -jnp.inf)
        pr = jax.nn.softmax(s, axis=-1)
        return jnp.einsum('bhqk,bkhd->bqhd', pr.astype(v.dtype), v)

    out = lax.map(one_block, (qn, qr, jnp.arange(nb)))
    return out.transpose(1, 0, 2, 3, 4).reshape(B, S, B_HEADS * B_V_DIM)


def fox_attention(q, k, v, logc):
    B, S = q.shape[0], q.shape[1]
    nb = S // QBLK
    qb = q.reshape(B, nb, QBLK, C_HEADS, C_HEAD_DIM).transpose(1, 0, 2, 3, 4)
    cb = logc.reshape(B, nb, QBLK, C_HEADS).transpose(1, 0, 2, 3)
    c_keys = logc.transpose(0, 2, 1)
    kpos = jnp.arange(S)

    def one_block(args):
        q_b, c_b, n = args
        s = jnp.einsum('bqhd,bkhd->bhqk', q_b, k).astype(jnp.float32) * (C_HEAD_DIM ** -0.5)
        s = s + c_b.transpose(0, 2, 1)[..., None] - c_keys[:, :, None, :]
        qpos = n * QBLK + jnp.arange(QBLK)
        s = jnp.where(kpos[None, :] <= qpos[:, None], s, -jnp.inf)
        pr = jax.nn.softmax(s, axis=-1)
        return jnp.einsum('bhqk,bkhd->bqhd', pr.astype(v.dtype), v)

    out = lax.map(one_block, (qb, cb, jnp.arange(nb)))
    return out.transpose(1, 0, 2, 3, 4).reshape(B, S, C_HEADS * C_HEAD_DIM)


def even_mixer(h, w_in, sinks, cq_norm, w_uq, ckv_norm, w_ukv, w_out):
    B, S = h.shape[0], h.shape[1]
    idx = [int(i) for i in np.cumsum(EVEN_IN_SPLITS)[:-1]]
    a_q, a_k, a_v, c_q, c_kv, k_rope = jnp.split(h @ w_in, idx, axis=-1)
    out_a = swa_sink_attention(a_q.reshape(B, S, A_HEADS, A_HEAD_DIM),
                               a_k.reshape(B, S, A_KV_HEADS, A_HEAD_DIM),
                               a_v.reshape(B, S, A_KV_HEADS, A_HEAD_DIM), sinks)
    q = (rms_norm(c_q, cq_norm) @ w_uq).reshape(B, S, B_HEADS, B_NOPE_DIM + B_ROPE_DIM)
    kv = (rms_norm(c_kv, ckv_norm) @ w_ukv).reshape(B, S, B_HEADS, B_NOPE_DIM + B_V_DIM)
    cos, sin = rope_tables(S, B_ROPE_DIM)
    q_nope = q[..., :B_NOPE_DIM]
    q_rope = apply_rope(q[..., B_NOPE_DIM:], cos[None, :, None], sin[None, :, None])
    k_nope = kv[..., :B_NOPE_DIM]
    v = kv[..., B_NOPE_DIM:]
    k_rope = apply_rope(k_rope, cos[None], sin[None])
    out_b = mla_attention(q_nope, q_rope, k_nope, k_rope, v)
    return jnp.concatenate([out_a, out_b], axis=-1) @ w_out


def odd_mixer(h, w_in, b_f, w_out):
    B, S = h.shape[0], h.shape[1]
    w = ODD_MIX_DIM
    q, k, v, f_logit = jnp.split(h @ w_in, [w, 2 * w, 3 * w], axis=-1)
    logf = jax.nn.log_sigmoid(f_logit.astype(jnp.float32) + b_f.astype(jnp.float32))
    logc = jnp.cumsum(logf, axis=1)
    shp = (B, S, C_HEADS, C_HEAD_DIM)
    out = fox_attention(q.reshape(shp), k.reshape(shp), v.reshape(shp), logc)
    return out @ w_out


def _normal(key, shape, scale):
    return jax.random.normal(key, shape, jnp.float32) * scale


def setup_inputs(seed: int = 0) -> dict:
    key = jax.random.key(seed)
    ks = jax.random.split(key, 23)
    D = D_MODEL
    return {
        'x': _normal(ks[0], (BATCH, SEQ, D), 1.0),
        'p': _normal(ks[1], (DEPTH, BATCH, SEQ, PLE_DIM), 1.0),
        'ffa_norm': 1.0 + _normal(ks[2], (DEPTH, D), 0.05),
        'ffa_w_gate_up': _normal(ks[3], (DEPTH, D, 2 * D_FF), D ** -0.5),
        'ffa_w_down': _normal(ks[4], (DEPTH, D_FF, D), D_FF ** -0.5),
        'mix_norm': 1.0 + _normal(ks[5], (DEPTH, D), 0.05),
        'ffb_norm': 1.0 + _normal(ks[6], (DEPTH, D), 0.05),
        'ffb_w_gate_up': _normal(ks[7], (DEPTH, D, 2 * D_FF), D ** -0.5),
        'ffb_w_down': _normal(ks[8], (DEPTH, D_FF, D), D_FF ** -0.5),
        'ple_norm': 1.0 + _normal(ks[9], (DEPTH, D), 0.05),
        'ple_w_gate': _normal(ks[10], (DEPTH, D, D), D ** -0.5),
        'ple_w_proj': _normal(ks[11], (DEPTH, PLE_DIM, D), PLE_DIM ** -0.5),
        'ev_w_in': _normal(ks[12], (N_EVEN, D, EVEN_IN_DIM), D ** -0.5),
        'ev_sinks': _normal(ks[13], (N_EVEN, A_HEADS), 0.5),
        'ev_cq_norm': 1.0 + _normal(ks[14], (N_EVEN, B_Q_LORA), 0.05),
        'ev_w_uq': _normal(ks[15], (N_EVEN, B_Q_LORA, B_HEADS * (B_NOPE_DIM + B_ROPE_DIM)), B_Q_LORA ** -0.5),
        'ev_ckv_norm': 1.0 + _normal(ks[16], (N_EVEN, B_KV_LORA), 0.05),
        'ev_w_ukv': _normal(ks[17], (N_EVEN, B_KV_LORA, B_HEADS * (B_NOPE_DIM + B_V_DIM)), B_KV_LORA ** -0.5),
        'ev_w_out': _normal(ks[18], (N_EVEN, EVEN_MIX_DIM, D), EVEN_MIX_DIM ** -0.5),
        'od_w_in': _normal(ks[19], (N_ODD, D, ODD_IN_DIM), D ** -0.5),
        'od_b_f': FORGET_BIAS_CENTER + _normal(ks[20], (N_ODD, C_HEADS), 0.5),
        'od_w_out': _normal(ks[21], (N_ODD, ODD_MIX_DIM, D), ODD_MIX_DIM ** -0.5),
        'final_norm': 1.0 + _normal(ks[22], (D,), 0.05),
    }


def reference(x, p, ffa_norm, ffa_w_gate_up, ffa_w_down, mix_norm, ffb_norm, ffb_w_gate_up, ffb_w_down,
              ple_norm, ple_w_gate, ple_w_proj, ev_w_in, ev_sinks, ev_cq_norm, ev_w_uq, ev_ckv_norm,
              ev_w_ukv, ev_w_out, od_w_in, od_b_f, od_w_out, final_norm):
    h = x
    for i in range(DEPTH):
        j = i // 2
        h = h + FFN_RES_SCALE * swiglu(rms_norm(h, ffa_norm[i]), ffa_w_gate_up[i], ffa_w_down[i])
        hn = rms_norm(h, mix_norm[i])
        if i % 2 == 0:
            h = h + even_mixer(hn, ev_w_in[j], ev_sinks[j], ev_cq_norm[j], ev_w_uq[j],
                               ev_ckv_norm[j], ev_w_ukv[j], ev_w_out[j])
        else:
            h = h + odd_mixer(hn, od_w_in[j], od_b_f[j], od_w_out[j])
        h = h + FFN_RES_SCALE * swiglu(rms_norm(h, ffb_norm[i]), ffb_w_gate_up[i], ffb_w_down[i])
        gate = jax.nn.sigmoid(rms_norm(h, ple_norm[i]) @ ple_w_gate[i])
        h = h + gate * (p[i] @ ple_w_proj[i])
    return rms_norm(h, final_norm)
```

```python
import functools

import numpy as np
import jax
import jax.numpy as jnp
from jax import lax
from jax.experimental import pallas as pl
from jax.experimental.pallas import tpu as pltpu

F32 = jnp.float32
BF16 = jnp.bfloat16

D_MODEL = 1024
D_FF = 2816
FFN_RES_SCALE = 0.5
RMS_EPS = 1e-6
PLE_DIM = 256
A_HEADS = 8
A_KV_HEADS = 2
A_HEAD_DIM = 64
WINDOW = 128
B_HEADS = 8
B_Q_LORA = 256
B_KV_LORA = 128
B_NOPE_DIM = 64
B_ROPE_DIM = 32
B_V_DIM = 64
ROPE_THETA = 10000.0
C_HEADS = 16
C_HEAD_DIM = 64

LANES = 128
HALF = LANES // 2
VMEM_LIMIT = 56 * 1024 * 1024

TOKEN_TILE = 512
FF_CHUNK = 256
ATT_TQ = 256
ATT_TK = 256
SWA_TQ = 512


def _resident(shape):
    nd = len(shape)
    return pl.BlockSpec(shape, lambda *_: (0,) * nd, pipeline_mode=pl.Buffered(1))


def _params(*sem):
    return pltpu.CompilerParams(dimension_semantics=sem, vmem_limit_bytes=VMEM_LIMIT)


def _rms(x, g):
    ms = jnp.mean(x * x, axis=-1, keepdims=True)
    return x * lax.rsqrt(ms + RMS_EPS) * g


def _dot(a, b):
    return jnp.dot(a, b, preferred_element_type=F32)


def _dot_nt(a, b):
    return lax.dot_general(a, b, (((1,), (1,)), ((), ())), preferred_element_type=F32)


def _ffn_body(h_ref, g_ref, wgu_ref, wd_ref, o_ref, act_ref):
    x = h_ref[...]
    xn = _rms(x, g_ref[...]).astype(BF16)
    for c in range(D_FF // FF_CHUNK):
        lo = c * FF_CHUNK
        gate = _dot(xn, wgu_ref[:, lo:lo + FF_CHUNK])
        up = _dot(xn, wgu_ref[:, D_FF + lo:D_FF + lo + FF_CHUNK])
        act_ref[:, lo:lo + FF_CHUNK] = (gate * jax.nn.sigmoid(gate) * up).astype(BF16)
    o_ref[...] = x + FFN_RES_SCALE * _dot(act_ref[...], wd_ref[...])


def _ffn(h, g, wgu, wd):
    t, d = h.shape
    tm = TOKEN_TILE
    row = pl.BlockSpec((tm, d), lambda i: (i, 0))
    return pl.pallas_call(
        _ffn_body,
        grid=(t // tm,),
        in_specs=[row, _resident(g.shape), _resident(wgu.shape), _resident(wd.shape)],
        out_specs=row,
        out_shape=jax.ShapeDtypeStruct((t, d), F32),
        scratch_shapes=[pltpu.VMEM((tm, D_FF), BF16)],
        compiler_params=_params("parallel"),
        name="ffn",
    )(h, g, wgu, wd)


def _resid_proj_body(*refs, n_in):
    h_ref = refs[0]
    o_ref = refs[1 + 2 * n_in]
    y = h_ref[...]
    for k in range(n_in):
        y = y + _dot(refs[1 + k][...], refs[1 + n_in + k][...])
    o_ref[...] = y


def _resid_proj(h, xs, ws):
    t, d = h.shape
    tm = TOKEN_TILE
    row = pl.BlockSpec((tm, d), lambda i: (i, 0))
    return pl.pallas_call(
        functools.partial(_resid_proj_body, n_in=len(xs)),
        grid=(t // tm,),
        in_specs=[row]
        + [pl.BlockSpec((tm, x.shape[1]), lambda i: (i, 0)) for x in xs]
        + [_resident(w.shape) for w in ws],
        out_specs=row,
        out_shape=jax.ShapeDtypeStruct((t, d), F32),
        compiler_params=_params("parallel"),
        name="resid_proj",
    )(h, *xs, *ws)


def _ple_body(h_ref, p_ref, g_ref, wg_ref, wp_ref, gf_ref, o_ref, *, final):
    x = h_ref[...]
    xn = _rms(x, g_ref[...]).astype(BF16)
    gate = jax.nn.sigmoid(_dot(xn, wg_ref[...]))
    y = x + gate * _dot(p_ref[...].astype(BF16), wp_ref[...])
    if final:
        y = _rms(y, gf_ref[...])
    o_ref[...] = y


def _ple(h, p_all, layer, g, wg, wp, gf, final):
    t, d = h.shape
    tm = TOKEN_TILE
    row = pl.BlockSpec((tm, d), lambda i: (i, 0))
    return pl.pallas_call(
        functools.partial(_ple_body, final=final),
        grid=(t // tm,),
        in_specs=[row, pl.BlockSpec((None, tm, PLE_DIM), lambda i: (layer, i, 0)),
                  _resident(g.shape), _resident(wg.shape), _resident(wp.shape), _resident(gf.shape)],
        out_specs=row,
        out_shape=jax.ShapeDtypeStruct((t, d), F32),
        compiler_params=_params("parallel"),
        name="ple",
    )(h, p_all, g, wg, wp, gf)


def _even_proj_body(h_ref, g_ref, win_ref, cqn_ref, wuq_ref, wuqs_ref, ckvn_ref, wk_ref, wv_ref,
                    cq_ref, sq_ref, ck_ref, sk_ref,
                    aq_ref, ak_ref, av_ref, mq_ref, mk_ref, mv_ref):
    xn = _rms(h_ref[...], g_ref[...]).astype(BF16)
    z = _dot(xn, win_ref[...])
    o_ak = A_HEADS * A_HEAD_DIM
    o_av = o_ak + LANES
    o_cq = o_av + LANES
    o_ckv = o_cq + B_Q_LORA
    o_kr = o_ckv + B_KV_LORA
    o_krs = o_kr + LANES
    aq_ref[...] = z[:, :o_ak].astype(BF16)
    ak_ref[...] = z[:, o_ak:o_av].astype(BF16)
    av_ref[...] = z[:, o_av:o_cq].astype(BF16)
    cqn = _rms(z[:, o_cq:o_ckv], cqn_ref[...]).astype(BF16)
    ckvn = _rms(z[:, o_ckv:o_kr], ckvn_ref[...]).astype(BF16)
    k_rope = z[:, o_kr:o_krs] * ck_ref[...] + z[:, o_krs:o_krs + LANES] * sk_ref[...]
    q_plain = _dot(cqn, wuq_ref[...])
    q_swap = _dot(cqn, wuqs_ref[...])
    k_nope = _dot(ckvn, wk_ref[...])
    cq = cq_ref[...]
    sq = sq_ref[...]
    for h in range(B_HEADS):
        sl = slice(h * LANES, (h + 1) * LANES)
        mq_ref[:, sl] = (q_plain[:, sl] * cq + q_swap[:, sl] * sq).astype(BF16)
        mk_ref[:, sl] = (k_nope[:, sl] + k_rope).astype(BF16)
    mv_ref[...] = _dot(ckvn, wv_ref[...]).astype(BF16)


def _even_proj(h, g, win, cqn, wuq, wuqs, ckvn, wk, wv, tabs, seq):
    t, d = h.shape
    tm = TOKEN_TILE
    n_pos = seq // tm
    row = lambda w: pl.BlockSpec((tm, w), lambda i: (i, 0))
    tab = pl.BlockSpec((tm, LANES), lambda i: (i % n_pos, 0))
    widths = (A_HEADS * A_HEAD_DIM, LANES, LANES, B_HEADS * LANES, B_HEADS * LANES, B_HEADS * B_V_DIM)
    return pl.pallas_call(
        _even_proj_body,
        grid=(t // tm,),
        in_specs=[row(d)] + [_resident(a.shape) for a in (g, win, cqn, wuq, wuqs, ckvn, wk, wv)] + [tab] * 4,
        out_specs=[row(w) for w in widths],
        out_shape=[jax.ShapeDtypeStruct((t, w), BF16) for w in widths],
        compiler_params=_params("parallel"),
        name="even_proj",
    )(h, g, win, cqn, wuq, wuqs, ckvn, wk, wv, *tabs)


def _odd_proj_body(h_ref, g_ref, win_ref, bf_ref, q_ref, k_ref, v_ref, c_ref, carry_ref):
    @pl.when(pl.program_id(1) == 0)
    def _():
        carry_ref[...] = jnp.zeros_like(carry_ref)

    xn = _rms(h_ref[0], g_ref[...]).astype(BF16)
    w = C_HEADS * C_HEAD_DIM
    q_ref[0] = _dot(xn, win_ref[:, 0:w]).astype(BF16)
    k_ref[0] = _dot(xn, win_ref[:, w:2 * w]).astype(BF16)
    v_ref[0] = _dot(xn, win_ref[:, 2 * w:3 * w]).astype(BF16)
    f = _dot(xn, win_ref[:, 3 * w:3 * w + LANES]) + bf_ref[...]
    c = jnp.minimum(f, 0.0) - jnp.log1p(jnp.exp(-jnp.abs(f)))
    tm = c.shape[0]
    rows = lax.broadcasted_iota(jnp.int32, c.shape, 0)
    shift = 1
    while shift < tm:
        c = c + jnp.where(rows >= shift, pltpu.roll(c, shift, axis=0), 0.0)
        shift *= 2
    c = c + carry_ref[...]
    c_ref[0] = c
    carry_ref[...] = c[tm - 1:tm, :]


def _odd_proj(h3, g, win, bf):
    b, s, d = h3.shape
    tm = TOKEN_TILE
    w = C_HEADS * C_HEAD_DIM
    blk = lambda n: pl.BlockSpec((1, tm, n), lambda bi, si: (bi, si, 0))
    return pl.pallas_call(
        _odd_proj_body,
        grid=(b, s // tm),
        in_specs=[blk(d), _resident(g.shape), _resident(win.shape), _resident(bf.shape)],
        out_specs=[blk(w), blk(w), blk(w), blk(LANES)],
        out_shape=[jax.ShapeDtypeStruct((b, s, w), BF16)] * 3 + [jax.ShapeDtypeStruct((b, s, LANES), F32)],
        scratch_shapes=[pltpu.VMEM((1, LANES), F32)],
        compiler_params=_params("parallel", "arbitrary"),
        name="odd_proj",
    )(h3, g, win, bf)


def _lane_halves(shape):
    lane = lax.broadcasted_iota(jnp.int32, shape, len(shape) - 1)
    return lane < HALF


def _split_pair(q):
    lo = _lane_halves(q.shape)
    qf = q.astype(F32)
    return jnp.where(lo, qf, 0.0).astype(BF16), jnp.where(lo, 0.0, qf).astype(BF16)


def _online_step(s, v_tile, state):
    m, l, acc = state
    m_new = jnp.maximum(m, jnp.max(s, axis=-1, keepdims=True))
    alpha = jnp.exp(m - m_new)
    p = jnp.exp(s - m_new)
    l = alpha * l + jnp.sum(p, axis=-1, keepdims=True)
    acc = alpha * acc + _dot(p.astype(BF16), v_tile)
    return m_new, l, acc


def _online_init(tq):
    return (jnp.full((tq, 1), -jnp.inf, F32), jnp.zeros((tq, 1), F32), jnp.zeros((tq, LANES), F32))


def _merge_pair(state_lo, state_hi):
    _, l0, a0 = state_lo
    _, l1, a1 = state_hi
    return jnp.where(_lane_halves(a0.shape), a0 / l0, a1 / l1)


def _causal_mask(tq, tk):
    row = lax.broadcasted_iota(jnp.int32, (tq, tk), 0)
    col = lax.broadcasted_iota(jnp.int32, (tq, tk), 1)
    return col <= row


def _mla_body(q_ref, k_ref, v_ref, o_ref):
    tq, tk = ATT_TQ, ATT_TK
    i = pl.program_id(2)
    qs = (q_ref[0, :, 0:LANES], q_ref[0, :, LANES:2 * LANES])
    causal = _causal_mask(tq, tk)

    def tile(j, states, masked):
        off = pl.multiple_of(j * tk, tk)
        k_tile = k_ref[0, pl.ds(off, tk), :]
        v_tile = v_ref[0, pl.ds(off, tk), :]
        out = []
        for hh in range(2):
            s = _dot_nt(qs[hh], k_tile[:, hh * LANES:(hh + 1) * LANES])
            if masked:
                s = jnp.where(causal, s, -jnp.inf)
            out.append(_online_step(s, v_tile, states[hh]))
        return tuple(out)

    states = (_online_init(tq), _online_init(tq))
    states = lax.fori_loop(0, i, lambda j, st: tile(j, st, False), states)
    states = tile(i, states, True)
    o_ref[0] = _merge_pair(*states).astype(BF16)


def _mla_attention(q, k, v):
    b, s, _ = q.shape
    tq = ATT_TQ
    n_pairs = B_HEADS // 2
    return pl.pallas_call(
        _mla_body,
        grid=(b, n_pairs, s // tq),
        in_specs=[pl.BlockSpec((1, tq, 2 * LANES), lambda bi, hp, qi: (bi, qi, hp)),
                  pl.BlockSpec((1, s, 2 * LANES), lambda bi, hp, qi: (bi, 0, hp)),
                  pl.BlockSpec((1, s, LANES), lambda bi, hp, qi: (bi, 0, hp))],
        out_specs=pl.BlockSpec((1, tq, LANES), lambda bi, hp, qi: (bi, qi, hp)),
        out_shape=jax.ShapeDtypeStruct((b, s, B_HEADS * B_V_DIM), BF16),
        compiler_params=_params("parallel", "parallel", "arbitrary"),
        name="mla_attention",
    )(q, k, v)


def _fox_body(q_ref, k_ref, v_ref, cq_ref, ck_ref, o_ref):
    tq, tk = ATT_TQ, ATT_TK
    i = pl.program_id(2)
    qs = _split_pair(q_ref[0])
    cqs = (cq_ref[0, 0, :, 0:1], cq_ref[0, 0, :, 1:2])
    causal = _causal_mask(tq, tk)

    def tile(j, states, masked):
        off = pl.multiple_of(j * tk, tk)
        k_tile = k_ref[0, pl.ds(off, tk), :]
        v_tile = v_ref[0, pl.ds(off, tk), :]
        out = []
        for hh in range(2):
            s = _dot_nt(qs[hh], k_tile) + cqs[hh] - ck_ref[0, 0, hh:hh + 1, pl.ds(off, tk)]
            if masked:
                s = jnp.where(causal, s, -jnp.inf)
            out.append(_online_step(s, v_tile, states[hh]))
        return tuple(out)

    states = (_online_init(tq), _online_init(tq))
    states = lax.fori_loop(0, i, lambda j, st: tile(j, st, False), states)
    states = tile(i, states, True)
    o_ref[0] = _merge_pair(*states).astype(BF16)


def _fox_attention(q, k, v, cq, ck):
    b, s, w = q.shape
    tq = ATT_TQ
    n_pairs = C_HEADS // 2
    pair_rows = pl.BlockSpec((1, tq, LANES), lambda bi, hp, qi: (bi, qi, hp))
    pair_seq = pl.BlockSpec((1, s, LANES), lambda bi, hp, qi: (bi, 0, hp))
    return pl.pallas_call(
        _fox_body,
        grid=(b, n_pairs, s // tq),
        in_specs=[pair_rows, pair_seq, pair_seq,
                  pl.BlockSpec((1, 1, tq, 2), lambda bi, hp, qi: (bi, hp, qi, 0)),
                  pl.BlockSpec((1, 1, 2, s), lambda bi, hp, qi: (bi, hp, 0, 0))],
        out_specs=pair_rows,
        out_shape=jax.ShapeDtypeStruct((b, s, w), BF16),
        compiler_params=_params("parallel", "parallel", "arbitrary"),
        name="fox_attention",
    )(q, k, v, cq, ck)


def _swa_body(sink_ref, q_ref, k_ref, v_ref, o_ref, *, slopes):
    w = WINDOW
    n_sub = SWA_TQ // w
    n_groups = A_HEADS // 2
    for sb in range(n_sub):
        blk = pl.program_id(1) * n_sub + sb
        k_start = pl.multiple_of(jnp.maximum(blk - 1, 0) * w, w)
        k_tile = k_ref[0, pl.ds(k_start, 2 * w), :]
        v_tile = v_ref[0, pl.ds(k_start, 2 * w), :]
        q_pos = blk * w + lax.broadcasted_iota(jnp.int32, (w, 2 * w), 0)
        k_pos = k_start + lax.broadcasted_iota(jnp.int32, (w, 2 * w), 1)
        dist = q_pos - k_pos
        band = (dist >= 0) & (dist < w)
        dist_f = dist.astype(F32)
        for grp in range(n_groups):
            q_pair = _split_pair(q_ref[0, sb * w:(sb + 1) * w, grp * LANES:(grp + 1) * LANES])
            pv = []
            for half in range(2):
                head = grp + n_groups * half
                s = _dot_nt(q_pair[half], k_tile) - slopes[head] * dist_f
                s = jnp.where(band, s, -jnp.inf)
                sink = sink_ref[head]
                m = jnp.maximum(jnp.max(s, axis=-1, keepdims=True), sink)
                e = jnp.exp(s - m)
                den = jnp.sum(e, axis=-1, keepdims=True) + jnp.exp(sink - m)
                pv.append(_dot((e / den).astype(BF16), v_tile))
            o_ref[0, sb * w:(sb + 1) * w, grp * LANES:(grp + 1) * LANES] = jnp.where(
                _lane_halves(pv[0].shape), pv[0], pv[1]).astype(BF16)


def _swa_attention(q, k, v, sinks):
    b, s, wq = q.shape
    tq = SWA_TQ
    slopes = tuple(float(2.0 ** (-8.0 * (h + 1) / A_HEADS)) for h in range(A_HEADS))
    rows = pl.BlockSpec((1, tq, wq), lambda bi, qi: (bi, qi, 0))
    seq = pl.BlockSpec((1, s, LANES), lambda bi, qi: (bi, 0, 0))
    return pl.pallas_call(
        functools.partial(_swa_body, slopes=slopes),
        grid=(b, s // tq),
        in_specs=[pl.BlockSpec(memory_space=pltpu.SMEM), rows, seq, seq],
        out_specs=rows,
        out_shape=jax.ShapeDtypeStruct((b, s, wq), BF16),
        compiler_params=_params("parallel", "arbitrary"),
        name="swa_attention",
    )(sinks, q, k, v)


def _swa_head_perm():
    n_groups = A_HEADS // 2
    idx = []
    for grp in range(n_groups):
        for head in (grp, grp + n_groups):
            idx.extend(range(head * A_HEAD_DIM, (head + 1) * A_HEAD_DIM))
    return np.asarray(idx, np.int32)


def _pad_cols(w, width):
    return jnp.pad(w, ((0, 0), (0, width - w.shape[1])))


def _even_weights(w_in, w_uq, w_ukv, w_out):
    perm = _swa_head_perm()
    nq = A_HEADS * A_HEAD_DIM
    nkv = A_KV_HEADS * A_HEAD_DIM
    o_cq = nq + 2 * nkv
    o_ckv = o_cq + B_Q_LORA
    o_kr = o_ckv + B_KV_LORA
    half = B_ROPE_DIM // 2
    aq = w_in[:, :nq][:, perm] * (A_HEAD_DIM ** -0.5)
    kr = w_in[:, o_kr:o_kr + B_ROPE_DIM]
    kr_swap = jnp.concatenate([kr[:, half:], kr[:, :half]], axis=1)
    place = lambda x: jnp.pad(x, ((0, 0), (B_NOPE_DIM, LANES - B_NOPE_DIM - B_ROPE_DIM)))
    win = jnp.concatenate([aq, w_in[:, nq:o_kr], place(kr), place(kr_swap)], axis=1).astype(BF16)

    dq = B_NOPE_DIM + B_ROPE_DIM
    uq = w_uq.reshape(B_Q_LORA, B_HEADS, dq)
    uq_swap = jnp.concatenate([jnp.zeros_like(uq[..., :B_NOPE_DIM]),
                               uq[..., B_NOPE_DIM + half:], uq[..., B_NOPE_DIM:B_NOPE_DIM + half]], axis=-1)
    pad_head = lambda x: jnp.pad(x, ((0, 0), (0, 0), (0, LANES - x.shape[-1]))).reshape(x.shape[0], B_HEADS * LANES)
    wuq = pad_head(uq).astype(BF16)
    wuqs = pad_head(uq_swap).astype(BF16)

    ukv = w_ukv.reshape(B_KV_LORA, B_HEADS, B_NOPE_DIM + B_V_DIM)
    wk = pad_head(ukv[..., :B_NOPE_DIM]).astype(BF16)
    wv = ukv[..., B_NOPE_DIM:].reshape(B_KV_LORA, B_HEADS * B_V_DIM).astype(BF16)

    wo_a = w_out[:nq][perm].astype(BF16)
    wo_b = w_out[nq:].astype(BF16)
    return win, wuq, wuqs, wk, wv, wo_a, wo_b


def _rope_tables(seq):
    half = B_ROPE_DIM // 2
    inv = ROPE_THETA ** (-jnp.arange(0, B_ROPE_DIM, 2, dtype=F32) / B_ROPE_DIM)
    ang = jnp.arange(seq, dtype=F32)[:, None] * inv[None, :]
    cos, sin = jnp.cos(ang), jnp.sin(ang)
    ones = jnp.ones((seq, B_NOPE_DIM), F32)
    zeros = jnp.zeros((seq, B_NOPE_DIM), F32)
    tail = jnp.zeros((seq, LANES - B_NOPE_DIM - B_ROPE_DIM), F32)
    c_tab = jnp.concatenate([ones, cos, cos, tail], axis=1)
    s_tab = jnp.concatenate([zeros, -sin, sin, tail], axis=1)
    scale = (B_NOPE_DIM + B_ROPE_DIM) ** -0.5
    return c_tab * scale, s_tab * scale, c_tab, s_tab


def _odd_weights(w_in, b_f):
    w = C_HEADS * C_HEAD_DIM
    q = w_in[:, :w] * (C_HEAD_DIM ** -0.5)
    win = jnp.concatenate([q, w_in[:, w:3 * w], _pad_cols(w_in[:, 3 * w:], LANES)], axis=1).astype(BF16)
    bf = _pad_cols(b_f.reshape(1, C_HEADS).astype(F32), LANES)
    return win, bf


def kernel(x, p, ffa_norm, ffa_w_gate_up, ffa_w_down, mix_norm, ffb_norm, ffb_w_gate_up, ffb_w_down, ple_norm, ple_w_gate, ple_w_proj, ev_w_in, ev_sinks, ev_cq_norm, ev_w_uq, ev_ckv_norm, ev_w_ukv, ev_w_out, od_w_in, od_b_f, od_w_out, final_norm):
    batch, seq, d = x.shape
    depth = p.shape[0]
    t = batch * seq
    h = x.reshape(t, d)
    p_all = p.reshape(depth, t, PLE_DIM)
    vec = lambda a: a.reshape(1, -1).astype(F32)
    tabs = _rope_tables(seq)

    for i in range(depth):
        j = i // 2
        h = _ffn(h, vec(ffa_norm[i]), ffa_w_gate_up[i].astype(BF16), ffa_w_down[i].astype(BF16))
        if i % 2 == 0:
            win, wuq, wuqs, wk, wv, wo_a, wo_b = _even_weights(ev_w_in[j], ev_w_uq[j], ev_w_ukv[j], ev_w_out[j])
            aq, ak, av, mq, mk, mv = _even_proj(h, vec(mix_norm[i]), win, vec(ev_cq_norm[j]), wuq, wuqs,
                                                vec(ev_ckv_norm[j]), wk, wv, tabs, seq)
            b3 = lambda a: a.reshape(batch, seq, a.shape[-1])
            out_a = _swa_attention(b3(aq), b3(ak), b3(av), ev_sinks[j].astype(F32))
            out_b = _mla_attention(b3(mq), b3(mk), b3(mv))
            h = _resid_proj(h, [out_a.reshape(t, -1), out_b.reshape(t, -1)], [wo_a, wo_b])
        else:
            win, bf = _odd_weights(od_w_in[j], od_b_f[j])
            q, k, v, logc = _odd_proj(h.reshape(batch, seq, d), vec(mix_norm[i]), win, bf)
            logc = logc[:, :, :C_HEADS]
            n_pairs = C_HEADS // 2
            cq = logc.reshape(batch, seq, n_pairs, 2).transpose(0, 2, 1, 3)
            ck = logc.reshape(batch, seq, n_pairs, 2).transpose(0, 2, 3, 1)
            out = _fox_attention(q, k, v, cq, ck)
            h = _resid_proj(h, [out.reshape(t, -1)], [od_w_out[j].astype(BF16)])
        h = _ffn(h, vec(ffb_norm[i]), ffb_w_gate_up[i].astype(BF16), ffb_w_down[i].astype(BF16))
        h = _ple(h, p_all, i, vec(ple_norm[i]), ple_w_gate[i].astype(BF16), ple_w_proj[i].astype(BF16),
                 vec(final_norm), final=(i == depth - 1))
    return h.reshape(batch, seq, d)
```

```python
import functools
import math

import numpy as np
import jax
import jax.numpy as jnp
from jax import lax
from jax.experimental import pallas as pl
from jax.experimental.pallas import tpu as pltpu

F32 = jnp.float32
BF16 = jnp.bfloat16

D_MODEL = 1024
D_FF = 2816
FFN_RES_SCALE = 0.5
RMS_EPS = 1e-6
PLE_DIM = 256
A_HEADS = 8
A_KV_HEADS = 2
A_HEAD_DIM = 64
WINDOW = 128
B_HEADS = 8
B_Q_LORA = 256
B_KV_LORA = 128
B_NOPE_DIM = 64
B_ROPE_DIM = 32
B_V_DIM = 64
ROPE_THETA = 10000.0
C_HEADS = 16
C_HEAD_DIM = 64

LOG2E = math.log2(math.e)
LANES = 128
HALF = LANES // 2
VMEM_LIMIT = 56 * 1024 * 1024

TOKEN_TILE = 512
FF_CHUNK = 256
ATT_TQ = 512
ATT_TK = 512
SWA_TQ = 512


def _resident(shape):
    nd = len(shape)
    return pl.BlockSpec(shape, lambda *_: (0,) * nd, pipeline_mode=pl.Buffered(1))


def _params(*sem):
    return pltpu.CompilerParams(dimension_semantics=sem, vmem_limit_bytes=VMEM_LIMIT)


def _rms(x, g):
    ms = jnp.mean(x * x, axis=-1, keepdims=True)
    return x * lax.rsqrt(ms + RMS_EPS) * g


def _dot(a, b):
    return jnp.dot(a, b, preferred_element_type=F32)


def _dot_nt(a, b):
    return lax.dot_general(a, b, (((1,), (1,)), ((), ())), preferred_element_type=F32)


def _lane_halves(shape):
    lane = lax.broadcasted_iota(jnp.int32, shape, len(shape) - 1)
    return lane < HALF


def _with_ones(v):
    lo = _lane_halves(v.shape)
    return jnp.where(lo, v, 1.0).astype(BF16), jnp.where(lo, 1.0, v).astype(BF16)


def _ffn_body(h_ref, g_ref, wgu_ref, wd_ref, o_ref, act_ref):
    x = h_ref[...]
    xn = _rms(x, g_ref[...]).astype(BF16)
    for c in range(D_FF // FF_CHUNK):
        lo = c * FF_CHUNK
        gate = _dot(xn, wgu_ref[:, lo:lo + FF_CHUNK])
        up = _dot(xn, wgu_ref[:, D_FF + lo:D_FF + lo + FF_CHUNK])
        act_ref[:, lo:lo + FF_CHUNK] = (gate * jax.nn.sigmoid(gate) * up).astype(BF16)
    o_ref[...] = x + FFN_RES_SCALE * _dot(act_ref[...], wd_ref[...])


def _ffn(h, g, wgu, wd):
    t, d = h.shape
    tm = TOKEN_TILE
    row = pl.BlockSpec((tm, d), lambda i: (i, 0))
    return pl.pallas_call(
        _ffn_body,
        grid=(t // tm,),
        in_specs=[row, _resident(g.shape), _resident(wgu.shape), _resident(wd.shape)],
        out_specs=row,
        out_shape=jax.ShapeDtypeStruct((t, d), F32),
        scratch_shapes=[pltpu.VMEM((tm, D_FF), BF16)],
        compiler_params=_params("parallel"),
        name="ffn",
    )(h, g, wgu, wd)


def _resid_proj_body(*refs, n_in):
    h_ref = refs[0]
    o_ref = refs[1 + 2 * n_in]
    y = h_ref[...]
    for k in range(n_in):
        y = y + _dot(refs[1 + k][...], refs[1 + n_in + k][...])
    o_ref[...] = y


def _resid_proj(h, xs, ws):
    t, d = h.shape
    tm = TOKEN_TILE
    row = pl.BlockSpec((tm, d), lambda i: (i, 0))
    return pl.pallas_call(
        functools.partial(_resid_proj_body, n_in=len(xs)),
        grid=(t // tm,),
        in_specs=[row]
        + [pl.BlockSpec((tm, x.shape[1]), lambda i: (i, 0)) for x in xs]
        + [_resident(w.shape) for w in ws],
        out_specs=row,
        out_shape=jax.ShapeDtypeStruct((t, d), F32),
        compiler_params=_params("parallel"),
        name="resid_proj",
    )(h, *xs, *ws)


def _ple_body(h_ref, p_ref, g_ref, wg_ref, wp_ref, gf_ref, o_ref, *, final):
    x = h_ref[...]
    xn = _rms(x, g_ref[...]).astype(BF16)
    gate = jax.nn.sigmoid(_dot(xn, wg_ref[...]))
    y = x + gate * _dot(p_ref[...].astype(BF16), wp_ref[...])
    if final:
        y = _rms(y, gf_ref[...])
    o_ref[...] = y


def _ple(h, p_all, layer, g, wg, wp, gf, final):
    t, d = h.shape
    tm = TOKEN_TILE
    row = pl.BlockSpec((tm, d), lambda i: (i, 0))
    return pl.pallas_call(
        functools.partial(_ple_body, final=final),
        grid=(t // tm,),
        in_specs=[row, pl.BlockSpec((None, tm, PLE_DIM), lambda i: (layer, i, 0)),
                  _resident(g.shape), _resident(wg.shape), _resident(wp.shape), _resident(gf.shape)],
        out_specs=row,
        out_shape=jax.ShapeDtypeStruct((t, d), F32),
        compiler_params=_params("parallel"),
        name="ple",
    )(h, p_all, g, wg, wp, gf)


def _even_proj_body(h_ref, g_ref, win_ref, cqn_ref, wuq_ref, wuqs_ref, ckvn_ref, wk_ref, wv_ref,
                    cq_ref, sq_ref, ck_ref, sk_ref,
                    aq_ref, ak_ref, av_ref, mq_ref, mk_ref, mvlo_ref, mvhi_ref):
    xn = _rms(h_ref[...], g_ref[...]).astype(BF16)
    z = _dot(xn, win_ref[...])
    o_ak = A_HEADS * A_HEAD_DIM
    o_av = o_ak + LANES
    o_cq = o_av + LANES
    o_ckv = o_cq + B_Q_LORA
    o_kr = o_ckv + B_KV_LORA
    o_krs = o_kr + LANES
    aq_ref[...] = z[:, :o_ak].astype(BF16)
    ak_ref[...] = z[:, o_ak:o_av].astype(BF16)
    av_ref[...] = z[:, o_av:o_cq].astype(BF16)
    cqn = _rms(z[:, o_cq:o_ckv], cqn_ref[...]).astype(BF16)
    ckvn = _rms(z[:, o_ckv:o_kr], ckvn_ref[...]).astype(BF16)
    k_rope = z[:, o_kr:o_krs] * ck_ref[...] + z[:, o_krs:o_krs + LANES] * sk_ref[...]
    q_plain = _dot(cqn, wuq_ref[...])
    q_swap = _dot(cqn, wuqs_ref[...])
    k_nope = _dot(ckvn, wk_ref[...])
    cq = cq_ref[...]
    sq = sq_ref[...]
    for h in range(B_HEADS):
        sl = slice(h * LANES, (h + 1) * LANES)
        mq_ref[:, sl] = (q_plain[:, sl] * cq + q_swap[:, sl] * sq).astype(BF16)
        mk_ref[:, sl] = (k_nope[:, sl] + k_rope).astype(BF16)
    v = _dot(ckvn, wv_ref[...])
    for grp in range(B_HEADS // 2):
        sl = slice(grp * LANES, (grp + 1) * LANES)
        mvlo_ref[:, sl], mvhi_ref[:, sl] = _with_ones(v[:, sl])


def _even_proj(h, g, win, cqn, wuq, wuqs, ckvn, wk, wv, tabs, seq):
    t, d = h.shape
    tm = TOKEN_TILE
    n_pos = seq // tm
    row = lambda w: pl.BlockSpec((tm, w), lambda i: (i, 0))
    tab = pl.BlockSpec((tm, LANES), lambda i: (i % n_pos, 0))
    widths = (A_HEADS * A_HEAD_DIM, LANES, LANES, B_HEADS * LANES, B_HEADS * LANES,
              B_HEADS * B_V_DIM, B_HEADS * B_V_DIM)
    return pl.pallas_call(
        _even_proj_body,
        grid=(t // tm,),
        in_specs=[row(d)] + [_resident(a.shape) for a in (g, win, cqn, wuq, wuqs, ckvn, wk, wv)] + [tab] * 4,
        out_specs=[row(w) for w in widths],
        out_shape=[jax.ShapeDtypeStruct((t, w), BF16) for w in widths],
        compiler_params=_params("parallel"),
        name="even_proj",
    )(h, g, win, cqn, wuq, wuqs, ckvn, wk, wv, *tabs)


def _odd_proj_body(h_ref, g_ref, win_ref, bf_ref, q_ref, k_ref, vlo_ref, vhi_ref, c_ref, carry_ref):
    @pl.when(pl.program_id(1) == 0)
    def _():
        carry_ref[...] = jnp.zeros_like(carry_ref)

    xn = _rms(h_ref[0], g_ref[...]).astype(BF16)
    w = C_HEADS * C_HEAD_DIM
    q_ref[0] = (_dot(xn, win_ref[:, 0:w]) * (C_HEAD_DIM ** -0.5 * LOG2E)).astype(BF16)
    k_ref[0] = _dot(xn, win_ref[:, w:2 * w]).astype(BF16)
    v = _dot(xn, win_ref[:, 2 * w:3 * w])
    for grp in range(C_HEADS // 2):
        sl = slice(grp * LANES, (grp + 1) * LANES)
        vlo_ref[0, :, sl], vhi_ref[0, :, sl] = _with_ones(v[:, sl])
    f = _dot(xn, win_ref[:, 3 * w:3 * w + LANES]) + bf_ref[...]
    c = jnp.minimum(f, 0.0) - jnp.log1p(jnp.exp(-jnp.abs(f)))
    tm = c.shape[0]
    rows = lax.broadcasted_iota(jnp.int32, c.shape, 0)
    shift = 1
    while shift < tm:
        c = c + jnp.where(rows >= shift, pltpu.roll(c, shift, axis=0), 0.0)
        shift *= 2
    c = c + carry_ref[...]
    c_ref[0] = c * LOG2E
    carry_ref[...] = c[tm - 1:tm, :]


def _odd_proj(h3, g, win, bf):
    b, s, d = h3.shape
    tm = TOKEN_TILE
    w = C_HEADS * C_HEAD_DIM
    blk = lambda n: pl.BlockSpec((1, tm, n), lambda bi, si: (bi, si, 0))
    return pl.pallas_call(
        _odd_proj_body,
        grid=(b, s // tm),
        in_specs=[blk(d), _resident(g.shape), _resident(win.shape), _resident(bf.shape)],
        out_specs=[blk(w)] * 4 + [blk(LANES)],
        out_shape=[jax.ShapeDtypeStruct((b, s, w), BF16)] * 4 + [jax.ShapeDtypeStruct((b, s, LANES), F32)],
        scratch_shapes=[pltpu.VMEM((1, LANES), F32)],
        compiler_params=_params("parallel", "arbitrary"),
        name="odd_proj",
    )(h3, g, win, bf)


def _split_pair(q):
    lo = _lane_halves(q.shape)
    qf = q.astype(F32)
    return jnp.where(lo, qf, 0.0).astype(BF16), jnp.where(lo, 0.0, qf).astype(BF16)


def _lane_chunks(s):
    return [s[:, c * LANES:(c + 1) * LANES] for c in range(s.shape[1] // LANES)]


def _online_step(chunks, v_ones, state):
    m_prev, acc = state
    m_new = jnp.maximum(m_prev, jnp.max(functools.reduce(jnp.maximum, chunks), axis=-1, keepdims=True))
    alpha = jnp.exp2(m_prev - m_new)
    p = jnp.concatenate([jnp.exp2(c - m_new).astype(BF16) for c in chunks], axis=1)
    return m_new, alpha * acc + _dot(p, v_ones)


def _merge_pair(acc_lo, acc_hi):
    n0 = acc_lo / pltpu.roll(acc_lo, HALF, axis=1)
    n1 = acc_hi / pltpu.roll(acc_hi, HALF, axis=1)
    return jnp.where(_lane_halves(acc_lo.shape), n0, n1)


def _visible(tq, tk, key_off):
    q_pos = pl.program_id(2) * tq + lax.broadcasted_iota(jnp.int32, (tq, tk), 0)
    return key_off + lax.broadcasted_iota(jnp.int32, (tq, tk), 1) <= q_pos


def _attend_pair(score_fn, v_refs, s_refs, m_ref, acc_ref):
    tq, tk = ATT_TQ, ATT_TK
    n_full = (pl.program_id(2) * tq) // tk
    m_ref[...] = jnp.full(m_ref.shape, -jnp.inf, F32)
    acc_ref[...] = jnp.zeros(acc_ref.shape, F32)

    def scores(j, s_ref):
        off = pl.multiple_of(j * tk, tk)
        for hh in range(2):
            s_ref[hh] = score_fn(hh, off)

    def consume(j, s_ref, masked):
        off = pl.multiple_of(j * tk, tk)
        for hh in range(2):
            s = s_ref[hh]
            if masked:
                s = jnp.where(_visible(tq, tk, off), s, -jnp.inf)
            v_ones = v_refs[hh][0, pl.ds(off, tk), :]
            m_ref[hh], acc_ref[hh] = _online_step(_lane_chunks(s), v_ones, (m_ref[hh], acc_ref[hh]))

    s_even, s_odd = s_refs
    scores(0, s_even)

    def two_tiles(pair, carry):
        j = 2 * pair
        scores(j + 1, s_odd)
        consume(j, s_even, False)
        scores(j + 2, s_even)
        consume(j + 1, s_odd, False)
        return carry

    lax.fori_loop(0, n_full // 2, two_tiles, 0)
    j = (n_full // 2) * 2

    @pl.when(n_full % 2 == 1)
    def _():
        scores(j + 1, s_odd)
        consume(j, s_even, False)
        consume(j + 1, s_odd, True)

    @pl.when(n_full % 2 == 0)
    def _():
        consume(j, s_even, True)

    return _merge_pair(acc_ref[0], acc_ref[1])


def _attend_scratch():
    tq, tk = ATT_TQ, ATT_TK
    return [pltpu.VMEM((2, tq, tk), F32), pltpu.VMEM((2, tq, tk), F32),
            pltpu.VMEM((2, tq, LANES), F32), pltpu.VMEM((2, tq, LANES), F32)]


def _mla_body(q_ref, k_ref, vlo_ref, vhi_ref, o_ref, s0_ref, s1_ref, m_ref, acc_ref):
    tk = ATT_TK

    def score_fn(hh, off):
        sl = slice(hh * LANES, (hh + 1) * LANES)
        return _dot_nt(q_ref[0, :, sl], k_ref[0, pl.ds(off, tk), sl])

    o_ref[0] = _attend_pair(score_fn, (vlo_ref, vhi_ref), (s0_ref, s1_ref), m_ref, acc_ref).astype(BF16)


def _mla_attention(q, k, v_lo, v_hi):
    b, s, _ = q.shape
    tq = ATT_TQ
    n_pairs = B_HEADS // 2
    pair_seq = pl.BlockSpec((1, s, LANES), lambda bi, hp, qi: (bi, 0, hp))
    return pl.pallas_call(
        _mla_body,
        grid=(b, n_pairs, s // tq),
        in_specs=[pl.BlockSpec((1, tq, 2 * LANES), lambda bi, hp, qi: (bi, qi, hp)),
                  pl.BlockSpec((1, s, 2 * LANES), lambda bi, hp, qi: (bi, 0, hp)),
                  pair_seq, pair_seq],
        out_specs=pl.BlockSpec((1, tq, LANES), lambda bi, hp, qi: (bi, qi, hp)),
        out_shape=jax.ShapeDtypeStruct((b, s, B_HEADS * B_V_DIM), BF16),
        scratch_shapes=_attend_scratch(),
        compiler_params=_params("parallel", "parallel", "arbitrary"),
        name="mla_attention",
    )(q, k, v_lo, v_hi)


def _fox_body(q_ref, k_ref, vlo_ref, vhi_ref, cq_ref, ck_ref, o_ref, s0_ref, s1_ref, m_ref, acc_ref):
    tq, tk = ATT_TQ, ATT_TK
    qs = _split_pair(q_ref[0])
    cqs = tuple(jnp.broadcast_to(cq_ref[0, 0, :, hh:hh + 1], (tq, LANES)) for hh in range(2))

    def score_fn(hh, off):
        s = _dot_nt(qs[hh], k_ref[0, pl.ds(off, tk), :])
        ck = ck_ref[0, 0, hh:hh + 1, pl.ds(off, tk)]
        return jnp.concatenate([c + cqs[hh] - ck[:, n * LANES:(n + 1) * LANES]
                                for n, c in enumerate(_lane_chunks(s))], axis=1)

    o_ref[0] = _attend_pair(score_fn, (vlo_ref, vhi_ref), (s0_ref, s1_ref), m_ref, acc_ref).astype(BF16)


def _fox_attention(q, k, v_lo, v_hi, cq, ck):
    b, s, w = q.shape
    tq = ATT_TQ
    n_pairs = C_HEADS // 2
    pair_rows = pl.BlockSpec((1, tq, LANES), lambda bi, hp, qi: (bi, qi, hp))
    pair_seq = pl.BlockSpec((1, s, LANES), lambda bi, hp, qi: (bi, 0, hp))
    return pl.pallas_call(
        _fox_body,
        grid=(b, n_pairs, s // tq),
        in_specs=[pair_rows, pair_seq, pair_seq, pair_seq,
                  pl.BlockSpec((1, 1, tq, 2), lambda bi, hp, qi: (bi, hp, qi, 0)),
                  pl.BlockSpec((1, 1, 2, s), lambda bi, hp, qi: (bi, hp, 0, 0))],
        out_specs=pair_rows,
        out_shape=jax.ShapeDtypeStruct((b, s, w), BF16),
        scratch_shapes=_attend_scratch(),
        compiler_params=_params("parallel", "parallel", "arbitrary"),
        name="fox_attention",
    )(q, k, v_lo, v_hi, cq, ck)


def _swa_body(sink_ref, q_ref, k_ref, v_ref, o_ref, *, slopes):
    w = WINDOW
    n_sub = SWA_TQ // w
    n_groups = A_HEADS // 2
    for sb in range(n_sub):
        blk = pl.program_id(1) * n_sub + sb
        k_start = pl.multiple_of(jnp.maximum(blk - 1, 0) * w, w)
        k_tile = k_ref[0, pl.ds(k_start, 2 * w), :]
        v_tile = v_ref[0, pl.ds(k_start, 2 * w), :]
        q_pos = blk * w + lax.broadcasted_iota(jnp.int32, (w, 2 * w), 0)
        k_pos = k_start + lax.broadcasted_iota(jnp.int32, (w, 2 * w), 1)
        dist = q_pos - k_pos
        band = (dist >= 0) & (dist < w)
        dist_f = dist.astype(F32)
        for grp in range(n_groups):
            q_pair = _split_pair(q_ref[0, sb * w:(sb + 1) * w, grp * LANES:(grp + 1) * LANES])
            pv = []
            for half in range(2):
                head = grp + n_groups * half
                s = _dot_nt(q_pair[half], k_tile) - slopes[head] * dist_f
                s = jnp.where(band, s, -jnp.inf)
                sink = sink_ref[head]
                m = jnp.maximum(jnp.max(s, axis=-1, keepdims=True), sink)
                e = jnp.exp(s - m)
                den = jnp.sum(e, axis=-1, keepdims=True) + jnp.exp(sink - m)
                pv.append(_dot((e / den).astype(BF16), v_tile))
            o_ref[0, sb * w:(sb + 1) * w, grp * LANES:(grp + 1) * LANES] = jnp.where(
                _lane_halves(pv[0].shape), pv[0], pv[1]).astype(BF16)


def _swa_attention(q, k, v, sinks):
    b, s, wq = q.shape
    tq = SWA_TQ
    slopes = tuple(float(2.0 ** (-8.0 * (h + 1) / A_HEADS)) for h in range(A_HEADS))
    rows = pl.BlockSpec((1, tq, wq), lambda bi, qi: (bi, qi, 0))
    seq = pl.BlockSpec((1, s, LANES), lambda bi, qi: (bi, 0, 0))
    return pl.pallas_call(
        functools.partial(_swa_body, slopes=slopes),
        grid=(b, s // tq),
        in_specs=[pl.BlockSpec(memory_space=pltpu.SMEM), rows, seq, seq],
        out_specs=rows,
        out_shape=jax.ShapeDtypeStruct((b, s, wq), BF16),
        compiler_params=_params("parallel", "arbitrary"),
        name="swa_attention",
    )(sinks, q, k, v)


def _swa_head_perm():
    n_groups = A_HEADS // 2
    idx = []
    for grp in range(n_groups):
        for head in (grp, grp + n_groups):
            idx.extend(range(head * A_HEAD_DIM, (head + 1) * A_HEAD_DIM))
    return np.asarray(idx, np.int32)


def _pad_cols(w, width):
    return jnp.pad(w, ((0, 0), (0, width - w.shape[1])))


def _even_weights(w_in, w_uq, w_ukv, w_out):
    perm = _swa_head_perm()
    nq = A_HEADS * A_HEAD_DIM
    nkv = A_KV_HEADS * A_HEAD_DIM
    o_cq = nq + 2 * nkv
    o_ckv = o_cq + B_Q_LORA
    o_kr = o_ckv + B_KV_LORA
    half = B_ROPE_DIM // 2
    aq = w_in[:, :nq][:, perm] * (A_HEAD_DIM ** -0.5)
    kr = w_in[:, o_kr:o_kr + B_ROPE_DIM]
    kr_swap = jnp.concatenate([kr[:, half:], kr[:, :half]], axis=1)
    place = lambda x: jnp.pad(x, ((0, 0), (B_NOPE_DIM, LANES - B_NOPE_DIM - B_ROPE_DIM)))
    win = jnp.concatenate([aq, w_in[:, nq:o_kr], place(kr), place(kr_swap)], axis=1).astype(BF16)

    dq = B_NOPE_DIM + B_ROPE_DIM
    uq = w_uq.reshape(B_Q_LORA, B_HEADS, dq)
    uq_swap = jnp.concatenate([jnp.zeros_like(uq[..., :B_NOPE_DIM]),
                               uq[..., B_NOPE_DIM + half:], uq[..., B_NOPE_DIM:B_NOPE_DIM + half]], axis=-1)
    pad_head = lambda x: jnp.pad(x, ((0, 0), (0, 0), (0, LANES - x.shape[-1]))).reshape(x.shape[0], B_HEADS * LANES)
    wuq = pad_head(uq).astype(BF16)
    wuqs = pad_head(uq_swap).astype(BF16)

    ukv = w_ukv.reshape(B_KV_LORA, B_HEADS, B_NOPE_DIM + B_V_DIM)
    wk = pad_head(ukv[..., :B_NOPE_DIM]).astype(BF16)
    wv = ukv[..., B_NOPE_DIM:].reshape(B_KV_LORA, B_HEADS * B_V_DIM).astype(BF16)

    wo_a = w_out[:nq][perm].astype(BF16)
    wo_b = w_out[nq:].astype(BF16)
    return win, wuq, wuqs, wk, wv, wo_a, wo_b


def _rope_tables(seq):
    half = B_ROPE_DIM // 2
    inv = ROPE_THETA ** (-jnp.arange(0, B_ROPE_DIM, 2, dtype=F32) / B_ROPE_DIM)
    ang = jnp.arange(seq, dtype=F32)[:, None] * inv[None, :]
    cos, sin = jnp.cos(ang), jnp.sin(ang)
    ones = jnp.ones((seq, B_NOPE_DIM), F32)
    zeros = jnp.zeros((seq, B_NOPE_DIM), F32)
    tail = jnp.zeros((seq, LANES - B_NOPE_DIM - B_ROPE_DIM), F32)
    c_tab = jnp.concatenate([ones, cos, cos, tail], axis=1)
    s_tab = jnp.concatenate([zeros, -sin, sin, tail], axis=1)
    scale = (B_NOPE_DIM + B_ROPE_DIM) ** -0.5 * LOG2E
    return c_tab * scale, s_tab * scale, c_tab, s_tab


def _odd_weights(w_in, b_f):
    w = C_HEADS * C_HEAD_DIM
    win = jnp.concatenate([w_in[:, :3 * w], _pad_cols(w_in[:, 3 * w:], LANES)], axis=1).astype(BF16)
    bf = _pad_cols(b_f.reshape(1, C_HEADS).astype(F32), LANES)
    return win, bf


def kernel(x, p, ffa_norm, ffa_w_gate_up, ffa_w_down, mix_norm, ffb_norm, ffb_w_gate_up, ffb_w_down, ple_norm, ple_w_gate, ple_w_proj, ev_w_in, ev_sinks, ev_cq_norm, ev_w_uq, ev_ckv_norm, ev_w_ukv, ev_w_out, od_w_in, od_b_f, od_w_out, final_norm):
    batch, seq, d = x.shape
    depth = p.shape[0]
    t = batch * seq
    h = x.reshape(t, d)
    p_all = p.reshape(depth, t, PLE_DIM)
    vec = lambda a: a.reshape(1, -1).astype(F32)
    tabs = _rope_tables(seq)

    for i in range(depth):
        j = i // 2
        h = _ffn(h, vec(ffa_norm[i]), ffa_w_gate_up[i].astype(BF16), ffa_w_down[i].astype(BF16))
        if i % 2 == 0:
            win, wuq, wuqs, wk, wv, wo_a, wo_b = _even_weights(ev_w_in[j], ev_w_uq[j], ev_w_ukv[j], ev_w_out[j])
            aq, ak, av, mq, mk, mv_lo, mv_hi = _even_proj(h, vec(mix_norm[i]), win, vec(ev_cq_norm[j]), wuq, wuqs,
                                                          vec(ev_ckv_norm[j]), wk, wv, tabs, seq)
            b3 = lambda a: a.reshape(batch, seq, a.shape[-1])
            out_a = _swa_attention(b3(aq), b3(ak), b3(av), ev_sinks[j].astype(F32))
            out_b = _mla_attention(b3(mq), b3(mk), b3(mv_lo), b3(mv_hi))
            h = _resid_proj(h, [out_a.reshape(t, -1), out_b.reshape(t, -1)], [wo_a, wo_b])
        else:
            win, bf = _odd_weights(od_w_in[j], od_b_f[j])
            q, k, v_lo, v_hi, logc = _odd_proj(h.reshape(batch, seq, d), vec(mix_norm[i]), win, bf)
            logc = logc[:, :, :C_HEADS]
            n_pairs = C_HEADS // 2
            cq = logc.reshape(batch, seq, n_pairs, 2).transpose(0, 2, 1, 3)
            ck = logc.reshape(batch, seq, n_pairs, 2).transpose(0, 2, 3, 1)
            out = _fox_attention(q, k, v_lo, v_hi, cq, ck)
            h = _resid_proj(h, [out.reshape(t, -1)], [od_w_out[j].astype(BF16)])
        h = _ffn(h, vec(ffb_norm[i]), ffb_w_gate_up[i].astype(BF16), ffb_w_down[i].astype(BF16))
        h = _ple(h, p_all, i, vec(ple_norm[i]), ple_w_gate[i].astype(BF16), ple_w_proj[i].astype(BF16),
                 vec(final_norm), final=(i == depth - 1))
    return h.reshape(batch, seq, d)
```

```python
import functools
import math

import numpy as np
import jax
import jax.numpy as jnp
from jax import lax
from jax.experimental import pallas as pl
from jax.experimental.pallas import tpu as pltpu

F32 = jnp.float32
BF16 = jnp.bfloat16

D_MODEL = 1024
D_FF = 2816
FFN_RES_SCALE = 0.5
RMS_EPS = 1e-6
PLE_DIM = 256
A_HEADS = 8
A_KV_HEADS = 2
A_HEAD_DIM = 64
WINDOW = 128
B_HEADS = 8
B_Q_LORA = 256
B_KV_LORA = 128
B_NOPE_DIM = 64
B_ROPE_DIM = 32
B_V_DIM = 64
ROPE_THETA = 10000.0
C_HEADS = 16
C_HEAD_DIM = 64

LOG2E = math.log2(math.e)
LANES = 128
HALF = LANES // 2
VMEM_LIMIT = 60 * 1024 * 1024

TOKEN_TILE = 512
FF_CHUNK = 256
ATT_TQ = 512
ATT_TK = 512
SWA_TQ = 512


def _resident(shape):
    nd = len(shape)
    return pl.BlockSpec(shape, lambda *_: (0,) * nd, pipeline_mode=pl.Buffered(1))


def _layer_resident(stack, layer):
    _, a, b = stack.shape
    return pl.BlockSpec((None, a, b), lambda *_: (layer, 0, 0), pipeline_mode=pl.Buffered(1))


def _params(*sem):
    return pltpu.CompilerParams(dimension_semantics=sem, vmem_limit_bytes=VMEM_LIMIT)


def _rms(x, g):
    ms = jnp.mean(x * x, axis=-1, keepdims=True)
    return x * lax.rsqrt(ms + RMS_EPS) * g


def _dot(a, b):
    return jnp.dot(a, b, preferred_element_type=F32)


def _dot_nt(a, b):
    return lax.dot_general(a, b, (((1,), (1,)), ((), ())), preferred_element_type=F32)


def _lane_halves(shape):
    lane = lax.broadcasted_iota(jnp.int32, shape, len(shape) - 1)
    return lane < HALF


def _with_ones(v):
    lo = _lane_halves(v.shape)
    return jnp.where(lo, v, 1.0).astype(BF16), jnp.where(lo, 1.0, v).astype(BF16)


def _ffn_half_step(x, g_ref, wgu_ref, wd_ref, act_ref):
    xn = _rms(x, g_ref[...]).astype(BF16)
    for c in range(D_FF // FF_CHUNK):
        lo = c * FF_CHUNK
        gate = _dot(xn, wgu_ref[:, lo:lo + FF_CHUNK])
        up = _dot(xn, wgu_ref[:, D_FF + lo:D_FF + lo + FF_CHUNK])
        act_ref[:, lo:lo + FF_CHUNK] = (gate * jax.nn.sigmoid(gate) * up).astype(BF16)
    return x + FFN_RES_SCALE * _dot(act_ref[...], wd_ref[...])


def _pre_even_body(h_ref, ga_ref, wgu_ref, wd_ref, gm_ref, win_ref, cqn_ref, wuq_ref, wuqs_ref,
                   ckvn_ref, wk_ref, wv_ref, cq_ref, sq_ref, ck_ref, sk_ref,
                   h1_ref, aq_ref, ak_ref, av_ref, mq_ref, mk_ref, mvlo_ref, mvhi_ref, act_ref):
    y = _ffn_half_step(h_ref[...], ga_ref, wgu_ref, wd_ref, act_ref)
    h1_ref[...] = y
    xn = _rms(y, gm_ref[...]).astype(BF16)
    z = _dot(xn, win_ref[...])
    o_ak = A_HEADS * A_HEAD_DIM
    o_av = o_ak + LANES
    o_cq = o_av + LANES
    o_ckv = o_cq + B_Q_LORA
    o_kr = o_ckv + B_KV_LORA
    o_krs = o_kr + LANES
    aq_ref[...] = z[:, :o_ak].astype(BF16)
    ak_ref[...] = z[:, o_ak:o_av].astype(BF16)
    av_ref[...] = z[:, o_av:o_cq].astype(BF16)
    cqn = _rms(z[:, o_cq:o_ckv], cqn_ref[...]).astype(BF16)
    ckvn = _rms(z[:, o_ckv:o_kr], ckvn_ref[...]).astype(BF16)
    k_rope = z[:, o_kr:o_krs] * ck_ref[...] + z[:, o_krs:o_krs + LANES] * sk_ref[...]
    q_plain = _dot(cqn, wuq_ref[...])
    q_swap = _dot(cqn, wuqs_ref[...])
    k_nope = _dot(ckvn, wk_ref[...])
    cq = cq_ref[...]
    sq = sq_ref[...]
    for h in range(B_HEADS):
        sl = slice(h * LANES, (h + 1) * LANES)
        mq_ref[:, sl] = (q_plain[:, sl] * cq + q_swap[:, sl] * sq).astype(BF16)
        mk_ref[:, sl] = (k_nope[:, sl] + k_rope).astype(BF16)
    v = _dot(ckvn, wv_ref[...])
    for grp in range(B_HEADS // 2):
        sl = slice(grp * LANES, (grp + 1) * LANES)
        mvlo_ref[:, sl], mvhi_ref[:, sl] = _with_ones(v[:, sl])


def _pre_even(h, layer, ga, wgu, wd, gm, small, tabs, seq):
    t, d = h.shape
    tm = TOKEN_TILE
    n_pos = seq // tm
    row = lambda w: pl.BlockSpec((tm, w), lambda i: (i, 0))
    tab = pl.BlockSpec((tm, LANES), lambda i: (i % n_pos, 0))
    widths = (A_HEADS * A_HEAD_DIM, LANES, LANES, B_HEADS * LANES, B_HEADS * LANES,
              B_HEADS * B_V_DIM, B_HEADS * B_V_DIM)
    stacks = (ga, wgu, wd, gm)
    return pl.pallas_call(
        _pre_even_body,
        grid=(t // tm,),
        in_specs=[row(d)] + [_layer_resident(a, layer) for a in stacks]
        + [_resident(a.shape) for a in small] + [tab] * 4,
        out_specs=[row(d)] + [row(w) for w in widths],
        out_shape=[jax.ShapeDtypeStruct((t, d), F32)] + [jax.ShapeDtypeStruct((t, w), BF16) for w in widths],
        scratch_shapes=[pltpu.VMEM((tm, D_FF), BF16)],
        compiler_params=_params("parallel"),
        name="pre_even",
    )(h, *stacks, *small, *tabs)


def _pre_odd_body(h_ref, ga_ref, wgu_ref, wd_ref, gm_ref, win_ref, bf_ref,
                  h1_ref, q_ref, k_ref, vlo_ref, vhi_ref, ct_ref, act_ref, carry_ref):
    @pl.when(pl.program_id(1) == 0)
    def _():
        carry_ref[...] = jnp.zeros_like(carry_ref)

    y = _ffn_half_step(h_ref[0], ga_ref, wgu_ref, wd_ref, act_ref)
    h1_ref[0] = y
    xn = _rms(y, gm_ref[...]).astype(BF16)
    w = C_HEADS * C_HEAD_DIM
    q_ref[0] = (_dot(xn, win_ref[:, 0:w]) * (C_HEAD_DIM ** -0.5 * LOG2E)).astype(BF16)
    k_ref[0] = _dot(xn, win_ref[:, w:2 * w]).astype(BF16)
    v = _dot(xn, win_ref[:, 2 * w:3 * w])
    for grp in range(C_HEADS // 2):
        sl = slice(grp * LANES, (grp + 1) * LANES)
        vlo_ref[0, :, sl], vhi_ref[0, :, sl] = _with_ones(v[:, sl])
    f = _dot(xn, win_ref[:, 3 * w:3 * w + LANES]) + bf_ref[...]
    c = jnp.minimum(f, 0.0) - jnp.log1p(jnp.exp(-jnp.abs(f)))
    tm = c.shape[0]
    rows = lax.broadcasted_iota(jnp.int32, c.shape, 0)
    shift = 1
    while shift < tm:
        c = c + jnp.where(rows >= shift, pltpu.roll(c, shift, axis=0), 0.0)
        shift *= 2
    c = c + carry_ref[...]
    carry_ref[...] = c[tm - 1:tm, :]
    ct_ref[0] = jnp.transpose(c * LOG2E)[:C_HEADS, :]


def _pre_odd(h3, layer, ga, wgu, wd, gm, win, bf):
    b, s, d = h3.shape
    tm = TOKEN_TILE
    w = C_HEADS * C_HEAD_DIM
    blk = lambda n: pl.BlockSpec((1, tm, n), lambda bi, si: (bi, si, 0))
    stacks = (ga, wgu, wd, gm)
    return pl.pallas_call(
        _pre_odd_body,
        grid=(b, s // tm),
        in_specs=[blk(d)] + [_layer_resident(a, layer) for a in stacks] + [_resident(win.shape), _resident(bf.shape)],
        out_specs=[blk(d)] + [blk(w)] * 4 + [pl.BlockSpec((1, C_HEADS, tm), lambda bi, si: (bi, 0, si))],
        out_shape=[jax.ShapeDtypeStruct((b, s, d), F32)] + [jax.ShapeDtypeStruct((b, s, w), BF16)] * 4
        + [jax.ShapeDtypeStruct((b, C_HEADS, s), F32)],
        scratch_shapes=[pltpu.VMEM((tm, D_FF), BF16), pltpu.VMEM((1, LANES), F32)],
        compiler_params=_params("parallel", "arbitrary"),
        name="pre_odd",
    )(h3, *stacks, win, bf)


def _post_body(*refs, n_in, final):
    h_ref = refs[0]
    x_refs = refs[1:1 + n_in]
    wo_refs = refs[1 + n_in:1 + 2 * n_in]
    gb_ref, wgu_ref, wd_ref, p_ref, gp_ref, wg_ref, wp_ref, gf_ref, o_ref, act_ref = refs[1 + 2 * n_in:]
    y = h_ref[...]
    for x_ref, wo_ref in zip(x_refs, wo_refs):
        y = y + _dot(x_ref[...], wo_ref[...])
    y = _ffn_half_step(y, gb_ref, wgu_ref, wd_ref, act_ref)
    gate = jax.nn.sigmoid(_dot(_rms(y, gp_ref[...]).astype(BF16), wg_ref[...]))
    y = y + gate * _dot(p_ref[...].astype(BF16), wp_ref[...])
    if final:
        y = _rms(y, gf_ref[...])
    o_ref[...] = y


def _post(h, xs, wos, layer, gb, wgu, wd, p_all, gp, wg, wp, gf, final):
    t, d = h.shape
    tm = TOKEN_TILE
    row = lambda w: pl.BlockSpec((tm, w), lambda i: (i, 0))
    return pl.pallas_call(
        functools.partial(_post_body, n_in=len(xs), final=final),
        grid=(t // tm,),
        in_specs=[row(d)] + [row(x.shape[1]) for x in xs] + [_resident(w.shape) for w in wos]
        + [_layer_resident(a, layer) for a in (gb, wgu, wd)]
        + [pl.BlockSpec((None, tm, PLE_DIM), lambda i: (layer, i, 0))]
        + [_layer_resident(a, layer) for a in (gp, wg, wp)] + [_resident(gf.shape)],
        out_specs=row(d),
        out_shape=jax.ShapeDtypeStruct((t, d), F32),
        scratch_shapes=[pltpu.VMEM((tm, D_FF), BF16)],
        compiler_params=_params("parallel"),
        name="post",
    )(h, *xs, *wos, gb, wgu, wd, p_all, gp, wg, wp, gf)


def _split_pair(q):
    lo = _lane_halves(q.shape)
    qf = q.astype(F32)
    return jnp.where(lo, qf, 0.0).astype(BF16), jnp.where(lo, 0.0, qf).astype(BF16)


def _lane_chunks(s):
    return [s[:, c * LANES:(c + 1) * LANES] for c in range(s.shape[1] // LANES)]


def _online_step(chunks, v_ones, state):
    m_prev, acc = state
    m_new = jnp.maximum(m_prev, jnp.max(functools.reduce(jnp.maximum, chunks), axis=-1, keepdims=True))
    alpha = jnp.exp2(m_prev - m_new)
    p = jnp.concatenate([jnp.exp2(c - m_new).astype(BF16) for c in chunks], axis=1)
    return m_new, alpha * acc + _dot(p, v_ones)


def _merge_pair(acc_lo, acc_hi):
    n0 = acc_lo / pltpu.roll(acc_lo, HALF, axis=1)
    n1 = acc_hi / pltpu.roll(acc_hi, HALF, axis=1)
    return jnp.where(_lane_halves(acc_lo.shape), n0, n1)


def _attend_pair(blk, score_fn, v_refs, s_refs, m_ref, acc_ref):
    tq, tk = ATT_TQ, ATT_TK
    n_full = (blk * tq) // tk
    m_ref[...] = jnp.full(m_ref.shape, -jnp.inf, F32)
    acc_ref[...] = jnp.zeros(acc_ref.shape, F32)

    def scores(j, s_ref):
        off = pl.multiple_of(j * tk, tk)
        for hh in range(2):
            s_ref[hh] = score_fn(hh, off)

    def consume(j, s_ref, masked):
        off = pl.multiple_of(j * tk, tk)
        for hh in range(2):
            s = s_ref[hh]
            if masked:
                q_pos = blk * tq + lax.broadcasted_iota(jnp.int32, (tq, tk), 0)
                k_pos = off + lax.broadcasted_iota(jnp.int32, (tq, tk), 1)
                s = jnp.where(k_pos <= q_pos, s, -jnp.inf)
            v_ones = v_refs[hh][0, pl.ds(off, tk), :]
            m_ref[hh], acc_ref[hh] = _online_step(_lane_chunks(s), v_ones, (m_ref[hh], acc_ref[hh]))

    s_even, s_odd = s_refs
    scores(0, s_even)

    def two_tiles(pair, carry):
        j = 2 * pair
        scores(j + 1, s_odd)
        consume(j, s_even, False)
        scores(j + 2, s_even)
        consume(j + 1, s_odd, False)
        return carry

    lax.fori_loop(0, n_full // 2, two_tiles, 0)
    j = (n_full // 2) * 2

    @pl.when(n_full % 2 == 1)
    def _():
        scores(j + 1, s_odd)
        consume(j, s_even, False)
        consume(j + 1, s_odd, True)

    @pl.when(n_full % 2 == 0)
    def _():
        consume(j, s_even, True)

    return _merge_pair(acc_ref[0], acc_ref[1])


def _attend_scratch():
    tq, tk = ATT_TQ, ATT_TK
    return [pltpu.VMEM((2, tq, tk), F32), pltpu.VMEM((2, tq, tk), F32),
            pltpu.VMEM((2, tq, LANES), F32), pltpu.VMEM((2, tq, LANES), F32)]


def _mla_body(q_ref, k_ref, vlo_ref, vhi_ref, o_ref, s0_ref, s1_ref, m_ref, acc_ref):
    tq, tk = ATT_TQ, ATT_TK

    def block(blk, carry):
        rows = pl.ds(pl.multiple_of(blk * tq, tq), tq)

        def score_fn(hh, off):
            sl = slice(hh * LANES, (hh + 1) * LANES)
            return _dot_nt(q_ref[0, rows, sl], k_ref[0, pl.ds(off, tk), sl])

        o_ref[0, rows, :] = _attend_pair(blk, score_fn, (vlo_ref, vhi_ref), (s0_ref, s1_ref),
                                         m_ref, acc_ref).astype(BF16)
        return carry

    lax.fori_loop(0, q_ref.shape[1] // tq, block, 0)


def _mla_attention(q, k, v_lo, v_hi):
    b, s, _ = q.shape
    n_pairs = B_HEADS // 2
    wide_seq = pl.BlockSpec((1, s, 2 * LANES), lambda bi, hp: (bi, 0, hp))
    pair_seq = pl.BlockSpec((1, s, LANES), lambda bi, hp: (bi, 0, hp))
    return pl.pallas_call(
        _mla_body,
        grid=(b, n_pairs),
        in_specs=[wide_seq, wide_seq, pair_seq, pair_seq],
        out_specs=pair_seq,
        out_shape=jax.ShapeDtypeStruct((b, s, B_HEADS * B_V_DIM), BF16),
        scratch_shapes=_attend_scratch(),
        compiler_params=_params("parallel", "parallel"),
        name="mla_attention",
    )(q, k, v_lo, v_hi)


def _fox_body(q_ref, k_ref, vlo_ref, vhi_ref, c_ref, o_ref, s0_ref, s1_ref, m_ref, acc_ref):
    tq, tk = ATT_TQ, ATT_TK

    def block(blk, carry):
        rows = pl.ds(pl.multiple_of(blk * tq, tq), tq)
        qs = _split_pair(q_ref[0, rows, :])
        cqs = tuple(jnp.transpose(jnp.broadcast_to(c_ref[0, 0, hh:hh + 1, rows], (LANES, tq))) for hh in range(2))

        def score_fn(hh, off):
            s = _dot_nt(qs[hh], k_ref[0, pl.ds(off, tk), :])
            ck = c_ref[0, 0, hh:hh + 1, pl.ds(off, tk)]
            return jnp.concatenate([c + cqs[hh] - ck[:, n * LANES:(n + 1) * LANES]
                                    for n, c in enumerate(_lane_chunks(s))], axis=1)

        o_ref[0, rows, :] = _attend_pair(blk, score_fn, (vlo_ref, vhi_ref), (s0_ref, s1_ref),
                                         m_ref, acc_ref).astype(BF16)
        return carry

    lax.fori_loop(0, q_ref.shape[1] // tq, block, 0)


def _fox_attention(q, k, v_lo, v_hi, c_t):
    b, s, w = q.shape
    n_pairs = C_HEADS // 2
    pair_seq = pl.BlockSpec((1, s, LANES), lambda bi, hp: (bi, 0, hp))
    return pl.pallas_call(
        _fox_body,
        grid=(b, n_pairs),
        in_specs=[pair_seq] * 4 + [pl.BlockSpec((1, 1, 2, s), lambda bi, hp: (bi, hp, 0, 0))],
        out_specs=pair_seq,
        out_shape=jax.ShapeDtypeStruct((b, s, w), BF16),
        scratch_shapes=_attend_scratch(),
        compiler_params=_params("parallel", "parallel"),
        name="fox_attention",
    )(q, k, v_lo, v_hi, c_t)


def _swa_body(sink_ref, q_ref, k_ref, v_ref, o_ref, *, slopes):
    w = WINDOW
    n_sub = SWA_TQ // w
    n_groups = A_HEADS // 2
    for sb in range(n_sub):
        blk = pl.program_id(1) * n_sub + sb
        k_start = pl.multiple_of(jnp.maximum(blk - 1, 0) * w, w)
        k_tile = k_ref[0, pl.ds(k_start, 2 * w), :]
        v_tile = v_ref[0, pl.ds(k_start, 2 * w), :]
        q_pos = blk * w + lax.broadcasted_iota(jnp.int32, (w, 2 * w), 0)
        k_pos = k_start + lax.broadcasted_iota(jnp.int32, (w, 2 * w), 1)
        dist = q_pos - k_pos
        band = (dist >= 0) & (dist < w)
        dist_f = dist.astype(F32)
        for grp in range(n_groups):
            q_pair = _split_pair(q_ref[0, sb * w:(sb + 1) * w, grp * LANES:(grp + 1) * LANES])
            pv = []
            for half in range(2):
                head = grp + n_groups * half
                s = _dot_nt(q_pair[half], k_tile) - slopes[head] * dist_f
                s = jnp.where(band, s, -jnp.inf)
                sink = sink_ref[head]
                m = jnp.maximum(jnp.max(s, axis=-1, keepdims=True), sink)
                e = jnp.exp(s - m)
                den = jnp.sum(e, axis=-1, keepdims=True) + jnp.exp(sink - m)
                pv.append(_dot((e / den).astype(BF16), v_tile))
            o_ref[0, sb * w:(sb + 1) * w, grp * LANES:(grp + 1) * LANES] = jnp.where(
                _lane_halves(pv[0].shape), pv[0], pv[1]).astype(BF16)


def _swa_attention(q, k, v, sinks):
    b, s, wq = q.shape
    tq = SWA_TQ
    slopes = tuple(float(2.0 ** (-8.0 * (h + 1) / A_HEADS)) for h in range(A_HEADS))
    rows = pl.BlockSpec((1, tq, wq), lambda bi, qi: (bi, qi, 0))
    seq = pl.BlockSpec((1, s, LANES), lambda bi, qi: (bi, 0, 0))
    return pl.pallas_call(
        functools.partial(_swa_body, slopes=slopes),
        grid=(b, s // tq),
        in_specs=[pl.BlockSpec(memory_space=pltpu.SMEM), rows, seq, seq],
        out_specs=rows,
        out_shape=jax.ShapeDtypeStruct((b, s, wq), BF16),
        compiler_params=_params("parallel", "arbitrary"),
        name="swa_attention",
    )(sinks, q, k, v)


def _swa_head_perm():
    n_groups = A_HEADS // 2
    idx = []
    for grp in range(n_groups):
        for head in (grp, grp + n_groups):
            idx.extend(range(head * A_HEAD_DIM, (head + 1) * A_HEAD_DIM))
    return np.asarray(idx, np.int32)


def _pad_cols(w, width):
    return jnp.pad(w, ((0, 0), (0, width - w.shape[1])))


def _even_weights(w_in, w_uq, w_ukv, w_out):
    perm = _swa_head_perm()
    nq = A_HEADS * A_HEAD_DIM
    nkv = A_KV_HEADS * A_HEAD_DIM
    o_cq = nq + 2 * nkv
    o_ckv = o_cq + B_Q_LORA
    o_kr = o_ckv + B_KV_LORA
    half = B_ROPE_DIM // 2
    aq = w_in[:, :nq][:, perm] * (A_HEAD_DIM ** -0.5)
    kr = w_in[:, o_kr:o_kr + B_ROPE_DIM]
    kr_swap = jnp.concatenate([kr[:, half:], kr[:, :half]], axis=1)
    place = lambda x: jnp.pad(x, ((0, 0), (B_NOPE_DIM, LANES - B_NOPE_DIM - B_ROPE_DIM)))
    win = jnp.concatenate([aq, w_in[:, nq:o_kr], place(kr), place(kr_swap)], axis=1).astype(BF16)

    dq = B_NOPE_DIM + B_ROPE_DIM
    uq = w_uq.reshape(B_Q_LORA, B_HEADS, dq)
    uq_swap = jnp.concatenate([jnp.zeros_like(uq[..., :B_NOPE_DIM]),
                               uq[..., B_NOPE_DIM + half:], uq[..., B_NOPE_DIM:B_NOPE_DIM + half]], axis=-1)
    pad_head = lambda x: jnp.pad(x, ((0, 0), (0, 0), (0, LANES - x.shape[-1]))).reshape(x.shape[0], B_HEADS * LANES)
    wuq = pad_head(uq).astype(BF16)
    wuqs = pad_head(uq_swap).astype(BF16)

    ukv = w_ukv.reshape(B_KV_LORA, B_HEADS, B_NOPE_DIM + B_V_DIM)
    wk = pad_head(ukv[..., :B_NOPE_DIM]).astype(BF16)
    wv = ukv[..., B_NOPE_DIM:].reshape(B_KV_LORA, B_HEADS * B_V_DIM).astype(BF16)

    wo_a = w_out[:nq][perm].astype(BF16)
    wo_b = w_out[nq:].astype(BF16)
    return win, wuq, wuqs, wk, wv, wo_a, wo_b


def _rope_tables(seq):
    inv = ROPE_THETA ** (-jnp.arange(0, B_ROPE_DIM, 2, dtype=F32) / B_ROPE_DIM)
    ang = jnp.arange(seq, dtype=F32)[:, None] * inv[None, :]
    cos, sin = jnp.cos(ang), jnp.sin(ang)
    ones = jnp.ones((seq, B_NOPE_DIM), F32)
    zeros = jnp.zeros((seq, B_NOPE_DIM), F32)
    tail = jnp.zeros((seq, LANES - B_NOPE_DIM - B_ROPE_DIM), F32)
    c_tab = jnp.concatenate([ones, cos, cos, tail], axis=1)
    s_tab = jnp.concatenate([zeros, -sin, sin, tail], axis=1)
    scale = (B_NOPE_DIM + B_ROPE_DIM) ** -0.5 * LOG2E
    return c_tab * scale, s_tab * scale, c_tab, s_tab


def _odd_weights(w_in, b_f):
    w = C_HEADS * C_HEAD_DIM
    win = jnp.concatenate([w_in[:, :3 * w], _pad_cols(w_in[:, 3 * w:], LANES)], axis=1).astype(BF16)
    bf = _pad_cols(b_f.reshape(1, C_HEADS).astype(F32), LANES)
    return win, bf


def kernel(x, p, ffa_norm, ffa_w_gate_up, ffa_w_down, mix_norm, ffb_norm, ffb_w_gate_up, ffb_w_down, ple_norm, ple_w_gate, ple_w_proj, ev_w_in, ev_sinks, ev_cq_norm, ev_w_uq, ev_ckv_norm, ev_w_ukv, ev_w_out, od_w_in, od_b_f, od_w_out, final_norm):
    batch, seq, d = x.shape
    depth = p.shape[0]
    t = batch * seq
    h = x.reshape(t, d)
    p_all = p.reshape(depth, t, PLE_DIM)
    vec = lambda a: a.reshape(1, -1).astype(F32)
    vecs = lambda a: a.reshape(a.shape[0], 1, a.shape[1]).astype(F32)
    tabs = _rope_tables(seq)
    ga, gm, gb, gp = vecs(ffa_norm), vecs(mix_norm), vecs(ffb_norm), vecs(ple_norm)
    wa_gu, wa_d = ffa_w_gate_up.astype(BF16), ffa_w_down.astype(BF16)
    wb_gu, wb_d = ffb_w_gate_up.astype(BF16), ffb_w_down.astype(BF16)
    wg, wp = ple_w_gate.astype(BF16), ple_w_proj.astype(BF16)
    gf = vec(final_norm)

    for i in range(depth):
        j = i // 2
        if i % 2 == 0:
            win, wuq, wuqs, wk, wv, wo_a, wo_b = _even_weights(ev_w_in[j], ev_w_uq[j], ev_w_ukv[j], ev_w_out[j])
            small = (win, vec(ev_cq_norm[j]), wuq, wuqs, vec(ev_ckv_norm[j]), wk, wv)
            h, aq, ak, av, mq, mk, mv_lo, mv_hi = _pre_even(h, i, ga, wa_gu, wa_d, gm, small, tabs, seq)
            b3 = lambda a: a.reshape(batch, seq, a.shape[-1])
            out_a = _swa_attention(b3(aq), b3(ak), b3(av), ev_sinks[j].astype(F32))
            out_b = _mla_attention(b3(mq), b3(mk), b3(mv_lo), b3(mv_hi))
            xs, wos = [out_a.reshape(t, -1), out_b.reshape(t, -1)], [wo_a, wo_b]
        else:
            win, bf = _odd_weights(od_w_in[j], od_b_f[j])
            h3, q, k, v_lo, v_hi, c_t = _pre_odd(h.reshape(batch, seq, d), i, ga, wa_gu, wa_d, gm, win, bf)
            h = h3.reshape(t, d)
            out = _fox_attention(q, k, v_lo, v_hi, c_t.reshape(batch, C_HEADS // 2, 2, seq))
            xs, wos = [out.reshape(t, -1)], [od_w_out[j].astype(BF16)]
        h = _post(h, xs, wos, i, gb, wb_gu, wb_d, p_all, gp, wg, wp, gf, final=(i == depth - 1))
    return h.reshape(batch, seq, d)
```

```python
import functools
import math

import numpy as np
import jax
import jax.numpy as jnp
from jax import lax
from jax.experimental import pallas as pl
from jax.experimental.pallas import tpu as pltpu

F32 = jnp.float32
BF16 = jnp.bfloat16

D_MODEL = 1024
D_FF = 2816
FFN_RES_SCALE = 0.5
RMS_EPS = 1e-6
PLE_DIM = 256
A_HEADS = 8
A_KV_HEADS = 2
A_HEAD_DIM = 64
WINDOW = 128
B_HEADS = 8
B_Q_LORA = 256
B_KV_LORA = 128
B_NOPE_DIM = 64
B_ROPE_DIM = 32
B_V_DIM = 64
ROPE_THETA = 10000.0
C_HEADS = 16
C_HEAD_DIM = 64

LOG2E = math.log2(math.e)
LANES = 128
HALF = LANES // 2
VMEM_LIMIT = 60 * 1024 * 1024

TOKEN_TILE = 512
FF_CHUNK = 256
ATT_TILE = 512
SWA_TQ = 512


def _resident(shape):
    nd = len(shape)
    return pl.BlockSpec(shape, lambda *_: (0,) * nd, pipeline_mode=pl.Buffered(1))


def _layer_resident(stack, layer):
    _, a, b = stack.shape
    return pl.BlockSpec((None, a, b), lambda *_: (layer, 0, 0), pipeline_mode=pl.Buffered(1))


def _params(*sem):
    return pltpu.CompilerParams(dimension_semantics=sem, vmem_limit_bytes=VMEM_LIMIT)


def _rms(x, g):
    ms = jnp.mean(x * x, axis=-1, keepdims=True)
    return x * lax.rsqrt(ms + RMS_EPS) * g


def _dot(a, b):
    return jnp.dot(a, b, preferred_element_type=F32)


def _dot_nt(a, b):
    return lax.dot_general(a, b, (((1,), (1,)), ((), ())), preferred_element_type=F32)


def _lane_halves(shape):
    lane = lax.broadcasted_iota(jnp.int32, shape, len(shape) - 1)
    return lane < HALF


def _with_ones(v):
    lo = _lane_halves(v.shape)
    return jnp.where(lo, v, 1.0).astype(BF16), jnp.where(lo, 1.0, v).astype(BF16)


def _ffn_half_step(x, g_ref, wgu_ref, wd_ref, act_ref):
    xn = _rms(x, g_ref[...]).astype(BF16)
    for c in range(D_FF // FF_CHUNK):
        lo = c * FF_CHUNK
        gate = _dot(xn, wgu_ref[:, lo:lo + FF_CHUNK])
        up = _dot(xn, wgu_ref[:, D_FF + lo:D_FF + lo + FF_CHUNK])
        act_ref[:, lo:lo + FF_CHUNK] = (gate * jax.nn.sigmoid(gate) * up).astype(BF16)
    return x + FFN_RES_SCALE * _dot(act_ref[...], wd_ref[...])


def _pre_even_body(h_ref, ga_ref, wgu_ref, wd_ref, gm_ref, win_ref, cqn_ref, wuq_ref, wuqs_ref,
                   ckvn_ref, wk_ref, wv_ref, cq_ref, sq_ref, ck_ref, sk_ref,
                   h1_ref, aq_ref, ak_ref, avlo_ref, avhi_ref, mq_ref, mk_ref, mvlo_ref, mvhi_ref, act_ref):
    y = _ffn_half_step(h_ref[...], ga_ref, wgu_ref, wd_ref, act_ref)
    h1_ref[...] = y
    xn = _rms(y, gm_ref[...]).astype(BF16)
    z = _dot(xn, win_ref[...])
    o_ak = A_HEADS * A_HEAD_DIM
    o_av = o_ak + LANES
    o_cq = o_av + LANES
    o_ckv = o_cq + B_Q_LORA
    o_kr = o_ckv + B_KV_LORA
    o_krs = o_kr + LANES
    aq_ref[...] = (z[:, :o_ak] * LOG2E).astype(BF16)
    ak_ref[...] = z[:, o_ak:o_av].astype(BF16)
    avlo_ref[...], avhi_ref[...] = _with_ones(z[:, o_av:o_cq])
    cqn = _rms(z[:, o_cq:o_ckv], cqn_ref[...]).astype(BF16)
    ckvn = _rms(z[:, o_ckv:o_kr], ckvn_ref[...]).astype(BF16)
    k_rope = z[:, o_kr:o_krs] * ck_ref[...] + z[:, o_krs:o_krs + LANES] * sk_ref[...]
    q_plain = _dot(cqn, wuq_ref[...])
    q_swap = _dot(cqn, wuqs_ref[...])
    k_nope = _dot(ckvn, wk_ref[...])
    cq = cq_ref[...]
    sq = sq_ref[...]
    for h in range(B_HEADS):
        sl = slice(h * LANES, (h + 1) * LANES)
        mq_ref[:, sl] = (q_plain[:, sl] * cq + q_swap[:, sl] * sq).astype(BF16)
        mk_ref[:, sl] = (k_nope[:, sl] + k_rope).astype(BF16)
    v = _dot(ckvn, wv_ref[...])
    for grp in range(B_HEADS // 2):
        sl = slice(grp * LANES, (grp + 1) * LANES)
        mvlo_ref[:, sl], mvhi_ref[:, sl] = _with_ones(v[:, sl])


def _pre_even(h, layer, ga, wgu, wd, gm, small, tabs, seq):
    t, d = h.shape
    tm = TOKEN_TILE
    n_pos = seq // tm
    row = lambda w: pl.BlockSpec((tm, w), lambda i: (i, 0))
    tab = pl.BlockSpec((tm, LANES), lambda i: (i % n_pos, 0))
    widths = (A_HEADS * A_HEAD_DIM, LANES, LANES, LANES, B_HEADS * LANES, B_HEADS * LANES,
              B_HEADS * B_V_DIM, B_HEADS * B_V_DIM)
    stacks = (ga, wgu, wd, gm)
    return pl.pallas_call(
        _pre_even_body,
        grid=(t // tm,),
        in_specs=[row(d)] + [_layer_resident(a, layer) for a in stacks]
        + [_resident(a.shape) for a in small] + [tab] * 4,
        out_specs=[row(d)] + [row(w) for w in widths],
        out_shape=[jax.ShapeDtypeStruct((t, d), F32)] + [jax.ShapeDtypeStruct((t, w), BF16) for w in widths],
        scratch_shapes=[pltpu.VMEM((tm, D_FF), BF16)],
        compiler_params=_params("parallel"),
        name="pre_even",
    )(h, *stacks, *small, *tabs)


def _pre_odd_body(h_ref, ga_ref, wgu_ref, wd_ref, gm_ref, win_ref, bf_ref,
                  h1_ref, q_ref, k_ref, vlo_ref, vhi_ref, ct_ref, act_ref, carry_ref):
    @pl.when(pl.program_id(1) == 0)
    def _():
        carry_ref[...] = jnp.zeros_like(carry_ref)

    y = _ffn_half_step(h_ref[0], ga_ref, wgu_ref, wd_ref, act_ref)
    h1_ref[0] = y
    xn = _rms(y, gm_ref[...]).astype(BF16)
    w = C_HEADS * C_HEAD_DIM
    q_ref[0] = (_dot(xn, win_ref[:, 0:w]) * (C_HEAD_DIM ** -0.5 * LOG2E)).astype(BF16)
    k_ref[0] = _dot(xn, win_ref[:, w:2 * w]).astype(BF16)
    v = _dot(xn, win_ref[:, 2 * w:3 * w])
    for grp in range(C_HEADS // 2):
        sl = slice(grp * LANES, (grp + 1) * LANES)
        vlo_ref[0, :, sl], vhi_ref[0, :, sl] = _with_ones(v[:, sl])
    f = _dot(xn, win_ref[:, 3 * w:3 * w + LANES]) + bf_ref[...]
    c = jnp.minimum(f, 0.0) - jnp.log1p(jnp.exp(-jnp.abs(f)))
    tm = c.shape[0]
    rows = lax.broadcasted_iota(jnp.int32, c.shape, 0)
    shift = 1
    while shift < tm:
        c = c + jnp.where(rows >= shift, pltpu.roll(c, shift, axis=0), 0.0)
        shift *= 2
    c = c + carry_ref[...]
    carry_ref[...] = c[tm - 1:tm, :]
    ct_ref[0] = jnp.transpose(c * LOG2E)[:C_HEADS, :]


def _pre_odd(h3, layer, ga, wgu, wd, gm, win, bf):
    b, s, d = h3.shape
    tm = TOKEN_TILE
    w = C_HEADS * C_HEAD_DIM
    blk = lambda n: pl.BlockSpec((1, tm, n), lambda bi, si: (bi, si, 0))
    stacks = (ga, wgu, wd, gm)
    return pl.pallas_call(
        _pre_odd_body,
        grid=(b, s // tm),
        in_specs=[blk(d)] + [_layer_resident(a, layer) for a in stacks] + [_resident(win.shape), _resident(bf.shape)],
        out_specs=[blk(d)] + [blk(w)] * 4 + [pl.BlockSpec((1, C_HEADS, tm), lambda bi, si: (bi, 0, si))],
        out_shape=[jax.ShapeDtypeStruct((b, s, d), F32)] + [jax.ShapeDtypeStruct((b, s, w), BF16)] * 4
        + [jax.ShapeDtypeStruct((b, C_HEADS, s), F32)],
        scratch_shapes=[pltpu.VMEM((tm, D_FF), BF16), pltpu.VMEM((1, LANES), F32)],
        compiler_params=_params("parallel", "arbitrary"),
        name="pre_odd",
    )(h3, *stacks, win, bf)


def _post_body(*refs, n_in, final):
    h_ref = refs[0]
    x_refs = refs[1:1 + n_in]
    wo_refs = refs[1 + n_in:1 + 2 * n_in]
    gb_ref, wgu_ref, wd_ref, p_ref, gp_ref, wg_ref, wp_ref, gf_ref, o_ref, act_ref = refs[1 + 2 * n_in:]
    y = h_ref[...]
    for x_ref, wo_ref in zip(x_refs, wo_refs):
        y = y + _dot(x_ref[...], wo_ref[...])
    y = _ffn_half_step(y, gb_ref, wgu_ref, wd_ref, act_ref)
    gate = jax.nn.sigmoid(_dot(_rms(y, gp_ref[...]).astype(BF16), wg_ref[...]))
    y = y + gate * _dot(p_ref[...].astype(BF16), wp_ref[...])
    if final:
        y = _rms(y, gf_ref[...])
    o_ref[...] = y


def _post(h, xs, wos, layer, gb, wgu, wd, p_all, gp, wg, wp, gf, final):
    t, d = h.shape
    tm = TOKEN_TILE
    row = lambda w: pl.BlockSpec((tm, w), lambda i: (i, 0))
    return pl.pallas_call(
        functools.partial(_post_body, n_in=len(xs), final=final),
        grid=(t // tm,),
        in_specs=[row(d)] + [row(x.shape[1]) for x in xs] + [_resident(w.shape) for w in wos]
        + [_layer_resident(a, layer) for a in (gb, wgu, wd)]
        + [pl.BlockSpec((None, tm, PLE_DIM), lambda i: (layer, i, 0))]
        + [_layer_resident(a, layer) for a in (gp, wg, wp)] + [_resident(gf.shape)],
        out_specs=row(d),
        out_shape=jax.ShapeDtypeStruct((t, d), F32),
        scratch_shapes=[pltpu.VMEM((tm, D_FF), BF16)],
        compiler_params=_params("parallel"),
        name="post",
    )(h, *xs, *wos, gb, wgu, wd, p_all, gp, wg, wp, gf)


def _split_pair(q):
    lo = _lane_halves(q.shape)
    qf = q.astype(F32)
    return jnp.where(lo, qf, 0.0).astype(BF16), jnp.where(lo, 0.0, qf).astype(BF16)


def _lane_chunks(s):
    return [s[:, c * LANES:(c + 1) * LANES] for c in range(s.shape[1] // LANES)]


def _online_step(chunks, v_ones, state):
    m_prev, acc = state
    m_new = jnp.maximum(m_prev, jnp.max(functools.reduce(jnp.maximum, chunks), axis=-1, keepdims=True))
    alpha = jnp.exp2(m_prev - m_new)
    p = jnp.concatenate([jnp.exp2(c - m_new).astype(BF16) for c in chunks], axis=1)
    return m_new, alpha * acc + _dot(p, v_ones)


def _merge_pair(acc_lo, acc_hi):
    n0 = acc_lo / pltpu.roll(acc_lo, HALF, axis=1)
    n1 = acc_hi / pltpu.roll(acc_hi, HALF, axis=1)
    return jnp.where(_lane_halves(acc_lo.shape), n0, n1)


def _attend_pair(blk, score_fn, v_refs, s_refs, m_ref, acc_ref):
    t = ATT_TILE
    half = t // 2
    m_ref[...] = jnp.full(m_ref.shape, -jnp.inf, F32)
    acc_ref[...] = jnp.zeros(acc_ref.shape, F32)

    def scores(j, s_ref):
        off = pl.multiple_of(j * t, t)
        for hh in range(2):
            s_ref[hh] = score_fn(hh, 0, t, off, t)

    def consume(j, s_ref):
        off = pl.multiple_of(j * t, t)
        for hh in range(2):
            v_ones = v_refs[hh][0, pl.ds(off, t), :]
            m_ref[hh], acc_ref[hh] = _online_step(_lane_chunks(s_ref[hh]), v_ones, (m_ref[hh], acc_ref[hh]))

    row_halves = ((0, half), (half, t))

    def scores_diagonal(s_ref):
        off = pl.multiple_of(blk * t, t)
        for hh in range(2):
            for row0, n_keys in row_halves:
                s_ref[hh, row0:row0 + half, 0:n_keys] = score_fn(hh, row0, half, off, n_keys)

    def consume_diagonal(s_ref):
        off = pl.multiple_of(blk * t, t)
        tri = (lax.broadcasted_iota(jnp.int32, (half, half), 1)
               <= lax.broadcasted_iota(jnp.int32, (half, half), 0))
        for hh in range(2):
            for row0, n_keys in row_halves:
                rows = slice(row0, row0 + half)
                square = jnp.where(tri, s_ref[hh, rows, n_keys - half:n_keys], -jnp.inf)
                chunks = _lane_chunks(s_ref[hh, rows, 0:n_keys - half]) + _lane_chunks(square)
                v_ones = v_refs[hh][0, pl.ds(off, n_keys), :]
                m_ref[hh, rows], acc_ref[hh, rows] = _online_step(chunks, v_ones, (m_ref[hh, rows], acc_ref[hh, rows]))

    s_even, s_odd = s_refs
    scores(0, s_even)

    def two_tiles(pair, carry):
        j = 2 * pair
        scores(j + 1, s_odd)
        consume(j, s_even)
        scores(j + 2, s_even)
        consume(j + 1, s_odd)
        return carry

    lax.fori_loop(0, blk // 2, two_tiles, 0)

    @pl.when(blk % 2 == 1)
    def _():
        scores_diagonal(s_odd)
        consume(blk - 1, s_even)
        consume_diagonal(s_odd)

    @pl.when(blk % 2 == 0)
    def _():
        consume_diagonal(s_even)

    return _merge_pair(acc_ref[0], acc_ref[1])


def _attend_scratch():
    t = ATT_TILE
    return [pltpu.VMEM((2, t, t), F32), pltpu.VMEM((2, t, t), F32),
            pltpu.VMEM((2, t, LANES), F32), pltpu.VMEM((2, t, LANES), F32)]


def _mla_body(q_ref, k_ref, vlo_ref, vhi_ref, o_ref, s0_ref, s1_ref, m_ref, acc_ref):
    t = ATT_TILE

    def block(blk, carry):
        base = pl.multiple_of(blk * t, t)

        def score_fn(hh, row0, n_rows, key0, n_keys):
            sl = slice(hh * LANES, (hh + 1) * LANES)
            rows = pl.ds(pl.multiple_of(base + row0, LANES), n_rows)
            return _dot_nt(q_ref[0, rows, sl], k_ref[0, pl.ds(key0, n_keys), sl])

        o_ref[0, pl.ds(base, t), :] = _attend_pair(blk, score_fn, (vlo_ref, vhi_ref), (s0_ref, s1_ref),
                                                   m_ref, acc_ref).astype(BF16)
        return carry

    lax.fori_loop(0, q_ref.shape[1] // t, block, 0)


def _mla_attention(q, k, v_lo, v_hi):
    b, s, _ = q.shape
    n_pairs = B_HEADS // 2
    wide_seq = pl.BlockSpec((1, s, 2 * LANES), lambda bi, hp: (bi, 0, hp))
    pair_seq = pl.BlockSpec((1, s, LANES), lambda bi, hp: (bi, 0, hp))
    return pl.pallas_call(
        _mla_body,
        grid=(b, n_pairs),
        in_specs=[wide_seq, wide_seq, pair_seq, pair_seq],
        out_specs=pair_seq,
        out_shape=jax.ShapeDtypeStruct((b, s, B_HEADS * B_V_DIM), BF16),
        scratch_shapes=_attend_scratch(),
        compiler_params=_params("parallel", "parallel"),
        name="mla_attention",
    )(q, k, v_lo, v_hi)


def _fox_body(q_ref, k_ref, vlo_ref, vhi_ref, c_ref, o_ref, s0_ref, s1_ref, m_ref, acc_ref):
    t = ATT_TILE

    def block(blk, carry):
        rows = pl.ds(pl.multiple_of(blk * t, t), t)
        qs = _split_pair(q_ref[0, rows, :])
        cqs = tuple(jnp.transpose(jnp.broadcast_to(c_ref[0, 0, hh:hh + 1, rows], (LANES, t))) for hh in range(2))

        def score_fn(hh, row0, n_rows, key0, n_keys):
            s = _dot_nt(qs[hh][row0:row0 + n_rows], k_ref[0, pl.ds(key0, n_keys), :])
            cq = cqs[hh][row0:row0 + n_rows]
            ck = c_ref[0, 0, hh:hh + 1, pl.ds(key0, n_keys)]
            return jnp.concatenate([c + cq - ck[:, n * LANES:(n + 1) * LANES]
                                    for n, c in enumerate(_lane_chunks(s))], axis=1)

        o_ref[0, rows, :] = _attend_pair(blk, score_fn, (vlo_ref, vhi_ref), (s0_ref, s1_ref),
                                         m_ref, acc_ref).astype(BF16)
        return carry

    lax.fori_loop(0, q_ref.shape[1] // t, block, 0)


def _fox_attention(q, k, v_lo, v_hi, c_t):
    b, s, w = q.shape
    n_pairs = C_HEADS // 2
    pair_seq = pl.BlockSpec((1, s, LANES), lambda bi, hp: (bi, 0, hp))
    return pl.pallas_call(
        _fox_body,
        grid=(b, n_pairs),
        in_specs=[pair_seq] * 4 + [pl.BlockSpec((1, 1, 2, s), lambda bi, hp: (bi, hp, 0, 0))],
        out_specs=pair_seq,
        out_shape=jax.ShapeDtypeStruct((b, s, w), BF16),
        scratch_shapes=_attend_scratch(),
        compiler_params=_params("parallel", "parallel"),
        name="fox_attention",
    )(q, k, v_lo, v_hi, c_t)


def _swa_body(sink_ref, q_ref, k_ref, vlo_ref, vhi_ref, bias_ref, o_ref):
    w = WINDOW
    n_sub = SWA_TQ // w
    n_groups = A_HEADS // 2
    v_refs = (vlo_ref, vhi_ref)
    for sb in range(n_sub):
        blk = pl.program_id(1) * n_sub + sb
        k_start = pl.multiple_of(jnp.maximum(blk - 1, 0) * w, w)
        table = jnp.minimum(blk, 1)
        k_tile = k_ref[0, pl.ds(k_start, 2 * w), :]
        heads = [(grp, half) for grp in range(n_groups) for half in range(2)]
        scores = []
        for grp, half in heads:
            q_half = _split_pair(q_ref[0, sb * w:(sb + 1) * w, grp * LANES:(grp + 1) * LANES])[half]
            scores.append(_dot_nt(q_half, k_tile) + bias_ref[table, grp + n_groups * half])
        probs = []
        for (grp, half), s in zip(heads, scores):
            chunks = _lane_chunks(s)
            sink = sink_ref[grp + n_groups * half]
            row_max = jnp.max(functools.reduce(jnp.maximum, chunks), axis=-1, keepdims=True)
            m = jnp.maximum(jnp.broadcast_to(row_max, (w, LANES)), sink)
            e = jnp.concatenate([jnp.exp2(c - m).astype(BF16) for c in chunks], axis=1)
            probs.append((e, jnp.exp2(sink - m)))
        out = []
        for (grp, half), (e, sink_term) in zip(heads, probs):
            pv = _dot(e, v_refs[half][0, pl.ds(k_start, 2 * w), :])
            out.append(pv / (pltpu.roll(pv, HALF, axis=1) + sink_term))
        for grp in range(n_groups):
            o_ref[0, sb * w:(sb + 1) * w, grp * LANES:(grp + 1) * LANES] = jnp.where(
                _lane_halves(out[0].shape), out[2 * grp], out[2 * grp + 1]).astype(BF16)


def _swa_bias_tables():
    w = WINDOW
    row = lax.broadcasted_iota(jnp.int32, (w, 2 * w), 0)
    col = lax.broadcasted_iota(jnp.int32, (w, 2 * w), 1)
    slopes = 2.0 ** (-8.0 * jnp.arange(1, A_HEADS + 1, dtype=F32) / A_HEADS)
    tables = []
    for dist in (row - col, row + w - col):
        band = (dist >= 0) & (dist < w)
        alibi = -LOG2E * slopes[:, None, None] * dist.astype(F32)[None]
        tables.append(jnp.where(band[None], alibi, -jnp.inf))
    return jnp.stack(tables)


def _swa_attention(q, k, v_lo, v_hi, sinks):
    b, s, wq = q.shape
    tq = SWA_TQ
    bias = _swa_bias_tables()
    rows = pl.BlockSpec((1, tq, wq), lambda bi, qi: (bi, qi, 0))
    seq = pl.BlockSpec((1, s, LANES), lambda bi, qi: (bi, 0, 0))
    return pl.pallas_call(
        _swa_body,
        grid=(b, s // tq),
        in_specs=[pl.BlockSpec(memory_space=pltpu.SMEM), rows, seq, seq, seq, _resident(bias.shape)],
        out_specs=rows,
        out_shape=jax.ShapeDtypeStruct((b, s, wq), BF16),
        compiler_params=_params("parallel", "arbitrary"),
        name="swa_attention",
    )(sinks, q, k, v_lo, v_hi, bias)


def _swa_head_perm():
    n_groups = A_HEADS // 2
    idx = []
    for grp in range(n_groups):
        for head in (grp, grp + n_groups):
            idx.extend(range(head * A_HEAD_DIM, (head + 1) * A_HEAD_DIM))
    return np.asarray(idx, np.int32)


def _pad_cols(w, width):
    return jnp.pad(w, ((0, 0), (0, width - w.shape[1])))


def _even_weights(w_in, w_uq, w_ukv, w_out):
    perm = _swa_head_perm()
    nq = A_HEADS * A_HEAD_DIM
    nkv = A_KV_HEADS * A_HEAD_DIM
    o_cq = nq + 2 * nkv
    o_ckv = o_cq + B_Q_LORA
    o_kr = o_ckv + B_KV_LORA
    half = B_ROPE_DIM // 2
    aq = w_in[:, :nq][:, perm] * (A_HEAD_DIM ** -0.5)
    kr = w_in[:, o_kr:o_kr + B_ROPE_DIM]
    kr_swap = jnp.concatenate([kr[:, half:], kr[:, :half]], axis=1)
    place = lambda x: jnp.pad(x, ((0, 0), (B_NOPE_DIM, LANES - B_NOPE_DIM - B_ROPE_DIM)))
    win = jnp.concatenate([aq, w_in[:, nq:o_kr], place(kr), place(kr_swap)], axis=1).astype(BF16)

    dq = B_NOPE_DIM + B_ROPE_DIM
    uq = w_uq.reshape(B_Q_LORA, B_HEADS, dq)
    uq_swap = jnp.concatenate([jnp.zeros_like(uq[..., :B_NOPE_DIM]),
                               uq[..., B_NOPE_DIM + half:], uq[..., B_NOPE_DIM:B_NOPE_DIM + half]], axis=-1)
    pad_head = lambda x: jnp.pad(x, ((0, 0), (0, 0), (0, LANES - x.shape[-1]))).reshape(x.shape[0], B_HEADS * LANES)
    wuq = pad_head(uq).astype(BF16)
    wuqs = pad_head(uq_swap).astype(BF16)

    ukv = w_ukv.reshape(B_KV_LORA, B_HEADS, B_NOPE_DIM + B_V_DIM)
    wk = pad_head(ukv[..., :B_NOPE_DIM]).astype(BF16)
    wv = ukv[..., B_NOPE_DIM:].reshape(B_KV_LORA, B_HEADS * B_V_DIM).astype(BF16)

    wo_a = w_out[:nq][perm].astype(BF16)
    wo_b = w_out[nq:].astype(BF16)
    return win, wuq, wuqs, wk, wv, wo_a, wo_b


def _rope_tables(seq):
    inv = ROPE_THETA ** (-jnp.arange(0, B_ROPE_DIM, 2, dtype=F32) / B_ROPE_DIM)
    ang = jnp.arange(seq, dtype=F32)[:, None] * inv[None, :]
    cos, sin = jnp.cos(ang), jnp.sin(ang)
    ones = jnp.ones((seq, B_NOPE_DIM), F32)
    zeros = jnp.zeros((seq, B_NOPE_DIM), F32)
    tail = jnp.zeros((seq, LANES - B_NOPE_DIM - B_ROPE_DIM), F32)
    c_tab = jnp.concatenate([ones, cos, cos, tail], axis=1)
    s_tab = jnp.concatenate([zeros, -sin, sin, tail], axis=1)
    scale = (B_NOPE_DIM + B_ROPE_DIM) ** -0.5 * LOG2E
    return c_tab * scale, s_tab * scale, c_tab, s_tab


def _odd_weights(w_in, b_f):
    w = C_HEADS * C_HEAD_DIM
    win = jnp.concatenate([w_in[:, :3 * w], _pad_cols(w_in[:, 3 * w:], LANES)], axis=1).astype(BF16)
    bf = _pad_cols(b_f.reshape(1, C_HEADS).astype(F32), LANES)
    return win, bf


def kernel(x, p, ffa_norm, ffa_w_gate_up, ffa_w_down, mix_norm, ffb_norm, ffb_w_gate_up, ffb_w_down, ple_norm, ple_w_gate, ple_w_proj, ev_w_in, ev_sinks, ev_cq_norm, ev_w_uq, ev_ckv_norm, ev_w_ukv, ev_w_out, od_w_in, od_b_f, od_w_out, final_norm):
    batch, seq, d = x.shape
    depth = p.shape[0]
    t = batch * seq
    h = x.reshape(t, d)
    p_all = p.reshape(depth, t, PLE_DIM)
    vec = lambda a: a.reshape(1, -1).astype(F32)
    vecs = lambda a: a.reshape(a.shape[0], 1, a.shape[1]).astype(F32)
    tabs = _rope_tables(seq)
    ga, gm, gb, gp = vecs(ffa_norm), vecs(mix_norm), vecs(ffb_norm), vecs(ple_norm)
    wa_gu, wa_d = ffa_w_gate_up.astype(BF16), ffa_w_down.astype(BF16)
    wb_gu, wb_d = ffb_w_gate_up.astype(BF16), ffb_w_down.astype(BF16)
    wg, wp = ple_w_gate.astype(BF16), ple_w_proj.astype(BF16)
    gf = vec(final_norm)

    for i in range(depth):
        j = i // 2
        if i % 2 == 0:
            win, wuq, wuqs, wk, wv, wo_a, wo_b = _even_weights(ev_w_in[j], ev_w_uq[j], ev_w_ukv[j], ev_w_out[j])
            small = (win, vec(ev_cq_norm[j]), wuq, wuqs, vec(ev_ckv_norm[j]), wk, wv)
            h, aq, ak, av_lo, av_hi, mq, mk, mv_lo, mv_hi = _pre_even(h, i, ga, wa_gu, wa_d, gm, small, tabs, seq)
            b3 = lambda a: a.reshape(batch, seq, a.shape[-1])
            out_a = _swa_attention(b3(aq), b3(ak), b3(av_lo), b3(av_hi), ev_sinks[j].astype(F32) * LOG2E)
            out_b = _mla_attention(b3(mq), b3(mk), b3(mv_lo), b3(mv_hi))
            xs, wos = [out_a.reshape(t, -1), out_b.reshape(t, -1)], [wo_a, wo_b]
        else:
            win, bf = _odd_weights(od_w_in[j], od_b_f[j])
            h3, q, k, v_lo, v_hi, c_t = _pre_odd(h.reshape(batch, seq, d), i, ga, wa_gu, wa_d, gm, win, bf)
            h = h3.reshape(t, d)
            out = _fox_attention(q, k, v_lo, v_hi, c_t.reshape(batch, C_HEADS // 2, 2, seq))
            xs, wos = [out.reshape(t, -1)], [od_w_out[j].astype(BF16)]
        h = _post(h, xs, wos, i, gb, wb_gu, wb_d, p_all, gp, wg, wp, gf, final=(i == depth - 1))
    return h.reshape(batch, seq, d)
```

```python
import functools
import math

import numpy as np
import jax
import jax.numpy as jnp
from jax import lax
from jax.experimental import pallas as pl
from jax.experimental.pallas import tpu as pltpu

F32 = jnp.float32
BF16 = jnp.bfloat16

D_MODEL = 1024
D_FF = 2816
FFN_RES_SCALE = 0.5
RMS_EPS = 1e-6
PLE_DIM = 256
A_HEADS = 8
A_KV_HEADS = 2
A_HEAD_DIM = 64
WINDOW = 128
B_HEADS = 8
B_Q_LORA = 256
B_KV_LORA = 128
B_NOPE_DIM = 64
B_ROPE_DIM = 32
B_V_DIM = 64
ROPE_THETA = 10000.0
C_HEADS = 16
C_HEAD_DIM = 64

LOG2E = math.log2(math.e)
LANES = 128
HALF = LANES // 2
VMEM_LIMIT = 60 * 1024 * 1024

TOKEN_TILE = 512
FF_CHUNK = 256
ATT_TILE = 512
ATT_PAIRS = 2
SWA_TQ = 512


def _resident(shape):
    nd = len(shape)
    return pl.BlockSpec(shape, lambda *_: (0,) * nd, pipeline_mode=pl.Buffered(1))


def _layer_resident(stack, layer):
    _, a, b = stack.shape
    return pl.BlockSpec((None, a, b), lambda *_: (layer, 0, 0), pipeline_mode=pl.Buffered(1))


def _params(*sem):
    return pltpu.CompilerParams(dimension_semantics=sem, vmem_limit_bytes=VMEM_LIMIT)


def _rms(x, g):
    ms = jnp.mean(x * x, axis=-1, keepdims=True)
    return x * lax.rsqrt(ms + RMS_EPS) * g


def _dot(a, b):
    return jnp.dot(a, b, preferred_element_type=F32)


def _dot_nt(a, b):
    return lax.dot_general(a, b, (((1,), (1,)), ((), ())), preferred_element_type=F32)


def _lane_halves(shape):
    lane = lax.broadcasted_iota(jnp.int32, shape, len(shape) - 1)
    return lane < HALF


def _pair_halves(x, fill):
    lo = _lane_halves(x.shape)
    return jnp.where(lo, x, fill).astype(BF16), jnp.where(lo, fill, x).astype(BF16)


def _with_ones(v):
    return _pair_halves(v, 1.0)


def _ffn_half_step(x, g_ref, wgu_ref, wd_ref, act_ref):
    xn = _rms(x, g_ref[...]).astype(BF16)
    for c in range(D_FF // FF_CHUNK):
        lo = c * FF_CHUNK
        gate = _dot(xn, wgu_ref[:, lo:lo + FF_CHUNK])
        up = _dot(xn, wgu_ref[:, D_FF + lo:D_FF + lo + FF_CHUNK])
        act_ref[:, lo:lo + FF_CHUNK] = (gate * jax.nn.sigmoid(gate) * up).astype(BF16)
    return x + FFN_RES_SCALE * _dot(act_ref[...], wd_ref[...])


def _pre_even_body(h_ref, ga_ref, wgu_ref, wd_ref, gm_ref, win_ref, cqn_ref, wuq_ref, wuqs_ref,
                   ckvn_ref, wk_ref, wv_ref, cq_ref, sq_ref, ck_ref, sk_ref,
                   h1_ref, aq_ref, ak_ref, avlo_ref, avhi_ref, mq_ref, mk_ref, mvlo_ref, mvhi_ref, act_ref):
    y = _ffn_half_step(h_ref[...], ga_ref, wgu_ref, wd_ref, act_ref)
    h1_ref[...] = y
    xn = _rms(y, gm_ref[...]).astype(BF16)
    z = _dot(xn, win_ref[...])
    o_ak = A_HEADS * A_HEAD_DIM
    o_av = o_ak + LANES
    o_cq = o_av + LANES
    o_ckv = o_cq + B_Q_LORA
    o_kr = o_ckv + B_KV_LORA
    o_krs = o_kr + LANES
    aq_ref[...] = (z[:, :o_ak] * LOG2E).astype(BF16)
    ak_ref[...] = z[:, o_ak:o_av].astype(BF16)
    avlo_ref[...], avhi_ref[...] = _with_ones(z[:, o_av:o_cq])
    cqn = _rms(z[:, o_cq:o_ckv], cqn_ref[...]).astype(BF16)
    ckvn = _rms(z[:, o_ckv:o_kr], ckvn_ref[...]).astype(BF16)
    k_rope = z[:, o_kr:o_krs] * ck_ref[...] + z[:, o_krs:o_krs + LANES] * sk_ref[...]
    q_plain = _dot(cqn, wuq_ref[...])
    q_swap = _dot(cqn, wuqs_ref[...])
    k_nope = _dot(ckvn, wk_ref[...])
    cq = cq_ref[...]
    sq = sq_ref[...]
    for h in range(B_HEADS):
        sl = slice(h * LANES, (h + 1) * LANES)
        mq_ref[:, sl] = (q_plain[:, sl] * cq + q_swap[:, sl] * sq).astype(BF16)
        mk_ref[:, sl] = (k_nope[:, sl] + k_rope).astype(BF16)
    v = _dot(ckvn, wv_ref[...])
    for grp in range(B_HEADS // 2):
        sl = slice(grp * LANES, (grp + 1) * LANES)
        mvlo_ref[:, sl], mvhi_ref[:, sl] = _with_ones(v[:, sl])


def _pre_even(h, layer, ga, wgu, wd, gm, small, tabs, seq):
    t, d = h.shape
    tm = TOKEN_TILE
    n_pos = seq // tm
    row = lambda w: pl.BlockSpec((tm, w), lambda i: (i, 0))
    tab = pl.BlockSpec((tm, LANES), lambda i: (i % n_pos, 0))
    widths = (A_HEADS * A_HEAD_DIM, LANES, LANES, LANES, B_HEADS * LANES, B_HEADS * LANES,
              B_HEADS * B_V_DIM, B_HEADS * B_V_DIM)
    stacks = (ga, wgu, wd, gm)
    return pl.pallas_call(
        _pre_even_body,
        grid=(t // tm,),
        in_specs=[row(d)] + [_layer_resident(a, layer) for a in stacks]
        + [_resident(a.shape) for a in small] + [tab] * 4,
        out_specs=[row(d)] + [row(w) for w in widths],
        out_shape=[jax.ShapeDtypeStruct((t, d), F32)] + [jax.ShapeDtypeStruct((t, w), BF16) for w in widths],
        scratch_shapes=[pltpu.VMEM((tm, D_FF), BF16)],
        compiler_params=_params("parallel"),
        name="pre_even",
    )(h, *stacks, *small, *tabs)


def _pre_odd_body(h_ref, ga_ref, wgu_ref, wd_ref, gm_ref, win_ref, bf_ref,
                  h1_ref, qlo_ref, qhi_ref, k_ref, vlo_ref, vhi_ref, ct_ref, act_ref, carry_ref):
    @pl.when(pl.program_id(1) == 0)
    def _():
        carry_ref[...] = jnp.zeros_like(carry_ref)

    y = _ffn_half_step(h_ref[0], ga_ref, wgu_ref, wd_ref, act_ref)
    h1_ref[0] = y
    xn = _rms(y, gm_ref[...]).astype(BF16)
    w = C_HEADS * C_HEAD_DIM
    q = _dot(xn, win_ref[:, 0:w]) * (C_HEAD_DIM ** -0.5 * LOG2E)
    k_ref[0] = _dot(xn, win_ref[:, w:2 * w]).astype(BF16)
    v = _dot(xn, win_ref[:, 2 * w:3 * w])
    for grp in range(C_HEADS // 2):
        sl = slice(grp * LANES, (grp + 1) * LANES)
        qlo_ref[0, :, sl], qhi_ref[0, :, sl] = _pair_halves(q[:, sl], 0.0)
        vlo_ref[0, :, sl], vhi_ref[0, :, sl] = _with_ones(v[:, sl])
    f = _dot(xn, win_ref[:, 3 * w:3 * w + LANES]) + bf_ref[...]
    c = jnp.minimum(f, 0.0) - jnp.log1p(jnp.exp(-jnp.abs(f)))
    tm = c.shape[0]
    rows = lax.broadcasted_iota(jnp.int32, c.shape, 0)
    shift = 1
    while shift < tm:
        c = c + jnp.where(rows >= shift, pltpu.roll(c, shift, axis=0), 0.0)
        shift *= 2
    c = c + carry_ref[...]
    carry_ref[...] = c[tm - 1:tm, :]
    ct_ref[0] = jnp.transpose(c * LOG2E)[:C_HEADS, :]


def _pre_odd(h3, layer, ga, wgu, wd, gm, win, bf):
    b, s, d = h3.shape
    tm = TOKEN_TILE
    w = C_HEADS * C_HEAD_DIM
    blk = lambda n: pl.BlockSpec((1, tm, n), lambda bi, si: (bi, si, 0))
    stacks = (ga, wgu, wd, gm)
    return pl.pallas_call(
        _pre_odd_body,
        grid=(b, s // tm),
        in_specs=[blk(d)] + [_layer_resident(a, layer) for a in stacks] + [_resident(win.shape), _resident(bf.shape)],
        out_specs=[blk(d)] + [blk(w)] * 5 + [pl.BlockSpec((1, C_HEADS, tm), lambda bi, si: (bi, 0, si))],
        out_shape=[jax.ShapeDtypeStruct((b, s, d), F32)] + [jax.ShapeDtypeStruct((b, s, w), BF16)] * 5
        + [jax.ShapeDtypeStruct((b, C_HEADS, s), F32)],
        scratch_shapes=[pltpu.VMEM((tm, D_FF), BF16), pltpu.VMEM((1, LANES), F32)],
        compiler_params=_params("parallel", "arbitrary"),
        name="pre_odd",
    )(h3, *stacks, win, bf)


def _post_body(*refs, n_in, final):
    h_ref = refs[0]
    x_refs = refs[1:1 + n_in]
    wo_refs = refs[1 + n_in:1 + 2 * n_in]
    gb_ref, wgu_ref, wd_ref, p_ref, gp_ref, wg_ref, wp_ref, gf_ref, o_ref, act_ref = refs[1 + 2 * n_in:]
    y = h_ref[...]
    for x_ref, wo_ref in zip(x_refs, wo_refs):
        y = y + _dot(x_ref[...], wo_ref[...])
    y = _ffn_half_step(y, gb_ref, wgu_ref, wd_ref, act_ref)
    gate = jax.nn.sigmoid(_dot(_rms(y, gp_ref[...]).astype(BF16), wg_ref[...]))
    y = y + gate * _dot(p_ref[...].astype(BF16), wp_ref[...])
    if final:
        y = _rms(y, gf_ref[...])
    o_ref[...] = y


def _post(h, xs, wos, layer, gb, wgu, wd, p_all, gp, wg, wp, gf, final):
    t, d = h.shape
    tm = TOKEN_TILE
    row = lambda w: pl.BlockSpec((tm, w), lambda i: (i, 0))
    return pl.pallas_call(
        functools.partial(_post_body, n_in=len(xs), final=final),
        grid=(t // tm,),
        in_specs=[row(d)] + [row(x.shape[1]) for x in xs] + [_resident(w.shape) for w in wos]
        + [_layer_resident(a, layer) for a in (gb, wgu, wd)]
        + [pl.BlockSpec((None, tm, PLE_DIM), lambda i: (layer, i, 0))]
        + [_layer_resident(a, layer) for a in (gp, wg, wp)] + [_resident(gf.shape)],
        out_specs=row(d),
        out_shape=jax.ShapeDtypeStruct((t, d), F32),
        scratch_shapes=[pltpu.VMEM((tm, D_FF), BF16)],
        compiler_params=_params("parallel"),
        name="post",
    )(h, *xs, *wos, gb, wgu, wd, p_all, gp, wg, wp, gf)


def _split_pair(q):
    lo = _lane_halves(q.shape)
    qf = q.astype(F32)
    return jnp.where(lo, qf, 0.0).astype(BF16), jnp.where(lo, 0.0, qf).astype(BF16)


def _lane_chunks(s):
    return [s[:, c * LANES:(c + 1) * LANES] for c in range(s.shape[1] // LANES)]


def _online_step(chunks, v_ones, state):
    m_prev, acc = state
    m_new = jnp.maximum(m_prev, jnp.max(functools.reduce(jnp.maximum, chunks), axis=-1, keepdims=True))
    alpha = jnp.exp2(m_prev - m_new)
    p = jnp.concatenate([jnp.exp2(c - m_new).astype(BF16) for c in chunks], axis=1)
    return m_new, alpha * acc + _dot(p, v_ones)


def _merge_pair(acc_lo, acc_hi):
    n0 = acc_lo / pltpu.roll(acc_lo, HALF, axis=1)
    n1 = acc_hi / pltpu.roll(acc_hi, HALF, axis=1)
    return jnp.where(_lane_halves(acc_lo.shape), n0, n1)


def _attend(blk, score_fn, v_fn, s_refs, m_ref, acc_ref):
    t = ATT_TILE
    half = t // 2
    heads = range(2 * ATT_PAIRS)
    m_ref[...] = jnp.full(m_ref.shape, -jnp.inf, F32)
    acc_ref[...] = jnp.zeros(acc_ref.shape, F32)

    def scores(j, s_ref):
        off = pl.multiple_of(j * t, t)
        for hh in heads:
            s_ref[hh] = score_fn(hh, 0, t, off, t)

    def consume(j, s_ref):
        off = pl.multiple_of(j * t, t)
        for hh in heads:
            m_ref[hh], acc_ref[hh] = _online_step(_lane_chunks(s_ref[hh]), v_fn(hh, off, t),
                                                  (m_ref[hh], acc_ref[hh]))

    row_halves = ((0, half), (half, t))

    def scores_diagonal(s_ref):
        off = pl.multiple_of(blk * t, t)
        for hh in heads:
            for row0, n_keys in row_halves:
                s_ref[hh, row0:row0 + half, 0:n_keys] = score_fn(hh, row0, half, off, n_keys)

    def consume_diagonal(s_ref):
        off = pl.multiple_of(blk * t, t)
        tri = (lax.broadcasted_iota(jnp.int32, (half, half), 1)
               <= lax.broadcasted_iota(jnp.int32, (half, half), 0))
        for hh in heads:
            for row0, n_keys in row_halves:
                rows = slice(row0, row0 + half)
                square = jnp.where(tri, s_ref[hh, rows, n_keys - half:n_keys], -jnp.inf)
                chunks = _lane_chunks(s_ref[hh, rows, 0:n_keys - half]) + _lane_chunks(square)
                m_ref[hh, rows], acc_ref[hh, rows] = _online_step(chunks, v_fn(hh, off, n_keys),
                                                                  (m_ref[hh, rows], acc_ref[hh, rows]))

    s_even, s_odd = s_refs
    scores(0, s_even)

    def two_tiles(pair, carry):
        j = 2 * pair
        scores(j + 1, s_odd)
        consume(j, s_even)
        scores(j + 2, s_even)
        consume(j + 1, s_odd)
        return carry

    lax.fori_loop(0, blk // 2, two_tiles, 0)

    @pl.when(blk % 2 == 1)
    def _():
        scores_diagonal(s_odd)
        consume(blk - 1, s_even)
        consume_diagonal(s_odd)

    @pl.when(blk % 2 == 0)
    def _():
        consume_diagonal(s_even)

    return [_merge_pair(acc_ref[2 * g], acc_ref[2 * g + 1]) for g in range(ATT_PAIRS)]


def _attend_scratch():
    t, n = ATT_TILE, 2 * ATT_PAIRS
    return [pltpu.VMEM((n, t, t), F32), pltpu.VMEM((n, t, t), F32),
            pltpu.VMEM((n, t, LANES), F32), pltpu.VMEM((n, t, LANES), F32)]


def _pair_values(vlo_ref, vhi_ref):
    def v_fn(hh, key0, n_keys):
        g, ref = hh // 2, (vlo_ref, vhi_ref)[hh % 2]
        return ref[0, pl.ds(key0, n_keys), g * LANES:(g + 1) * LANES]
    return v_fn


def _mla_body(q_ref, k_ref, vlo_ref, vhi_ref, o_ref, s0_ref, s1_ref, m_ref, acc_ref):
    t = ATT_TILE

    def block(blk, carry):
        base = pl.multiple_of(blk * t, t)

        def score_fn(hh, row0, n_rows, key0, n_keys):
            sl = slice(hh * LANES, (hh + 1) * LANES)
            rows = pl.ds(pl.multiple_of(base + row0, LANES), n_rows)
            return _dot_nt(q_ref[0, rows, sl], k_ref[0, pl.ds(key0, n_keys), sl])

        outs = _attend(blk, score_fn, _pair_values(vlo_ref, vhi_ref), (s0_ref, s1_ref), m_ref, acc_ref)
        for g, out in enumerate(outs):
            o_ref[0, pl.ds(base, t), g * LANES:(g + 1) * LANES] = out.astype(BF16)
        return carry

    lax.fori_loop(0, q_ref.shape[1] // t, block, 0)


def _mla_attention(q, k, v_lo, v_hi):
    b, s, _ = q.shape
    n_steps = B_HEADS // (2 * ATT_PAIRS)
    per_head = pl.BlockSpec((1, s, 2 * ATT_PAIRS * LANES), lambda bi, hg: (bi, 0, hg))
    per_pair = pl.BlockSpec((1, s, ATT_PAIRS * LANES), lambda bi, hg: (bi, 0, hg))
    return pl.pallas_call(
        _mla_body,
        grid=(b, n_steps),
        in_specs=[per_head, per_head, per_pair, per_pair],
        out_specs=per_pair,
        out_shape=jax.ShapeDtypeStruct((b, s, B_HEADS * B_V_DIM), BF16),
        scratch_shapes=_attend_scratch(),
        compiler_params=_params("parallel", "parallel"),
        name="mla_attention",
    )(q, k, v_lo, v_hi)


def _fox_body(qlo_ref, qhi_ref, k_ref, vlo_ref, vhi_ref, c_ref, o_ref, s0_ref, s1_ref, m_ref, acc_ref):
    t = ATT_TILE

    def block(blk, carry):
        base = pl.multiple_of(blk * t, t)
        rows = pl.ds(base, t)
        cqs = [jnp.transpose(jnp.broadcast_to(c_ref[0, hh // 2, hh % 2:hh % 2 + 1, rows], (LANES, t)))
               for hh in range(2 * ATT_PAIRS)]

        def score_fn(hh, row0, n_rows, key0, n_keys):
            g, q_ref = hh // 2, (qlo_ref, qhi_ref)[hh % 2]
            lanes = slice(g * LANES, (g + 1) * LANES)
            q = q_ref[0, pl.ds(pl.multiple_of(base + row0, LANES), n_rows), lanes]
            s = _dot_nt(q, k_ref[0, pl.ds(key0, n_keys), lanes])
            cq = cqs[hh][row0:row0 + n_rows]
            ck = c_ref[0, g, hh % 2:hh % 2 + 1, pl.ds(key0, n_keys)]
            return jnp.concatenate([c + cq - ck[:, n * LANES:(n + 1) * LANES]
                                    for n, c in enumerate(_lane_chunks(s))], axis=1)

        outs = _attend(blk, score_fn, _pair_values(vlo_ref, vhi_ref), (s0_ref, s1_ref), m_ref, acc_ref)
        for g, out in enumerate(outs):
            o_ref[0, rows, g * LANES:(g + 1) * LANES] = out.astype(BF16)
        return carry

    lax.fori_loop(0, k_ref.shape[1] // t, block, 0)


def _fox_attention(q_lo, q_hi, k, v_lo, v_hi, c_t):
    b, s, w = k.shape
    n_steps = C_HEADS // (2 * ATT_PAIRS)
    per_pair = pl.BlockSpec((1, s, ATT_PAIRS * LANES), lambda bi, hg: (bi, 0, hg))
    return pl.pallas_call(
        _fox_body,
        grid=(b, n_steps),
        in_specs=[per_pair] * 5 + [pl.BlockSpec((1, ATT_PAIRS, 2, s), lambda bi, hg: (bi, hg, 0, 0))],
        out_specs=per_pair,
        out_shape=jax.ShapeDtypeStruct((b, s, w), BF16),
        scratch_shapes=_attend_scratch(),
        compiler_params=_params("parallel", "parallel"),
        name="fox_attention",
    )(q_lo, q_hi, k, v_lo, v_hi, c_t)


def _swa_body(sink_ref, q_ref, k_ref, vlo_ref, vhi_ref, bias_ref, o_ref):
    w = WINDOW
    n_sub = SWA_TQ // w
    n_groups = A_HEADS // 2
    v_refs = (vlo_ref, vhi_ref)
    for sb in range(n_sub):
        blk = pl.program_id(1) * n_sub + sb
        k_start = pl.multiple_of(jnp.maximum(blk - 1, 0) * w, w)
        table = jnp.minimum(blk, 1)
        k_tile = k_ref[0, pl.ds(k_start, 2 * w), :]
        heads = [(grp, half) for grp in range(n_groups) for half in range(2)]
        scores = []
        for grp, half in heads:
            q_half = _split_pair(q_ref[0, sb * w:(sb + 1) * w, grp * LANES:(grp + 1) * LANES])[half]
            scores.append(_dot_nt(q_half, k_tile) + bias_ref[table, grp + n_groups * half])
        probs = []
        for (grp, half), s in zip(heads, scores):
            chunks = _lane_chunks(s)
            sink = sink_ref[grp + n_groups * half]
            row_max = jnp.max(functools.reduce(jnp.maximum, chunks), axis=-1, keepdims=True)
            m = jnp.maximum(jnp.broadcast_to(row_max, (w, LANES)), sink)
            e = jnp.concatenate([jnp.exp2(c - m).astype(BF16) for c in chunks], axis=1)
            probs.append((e, jnp.exp2(sink - m)))
        out = []
        for (grp, half), (e, sink_term) in zip(heads, probs):
            pv = _dot(e, v_refs[half][0, pl.ds(k_start, 2 * w), :])
            out.append(pv / (pltpu.roll(pv, HALF, axis=1) + sink_term))
        for grp in range(n_groups):
            o_ref[0, sb * w:(sb + 1) * w, grp * LANES:(grp + 1) * LANES] = jnp.where(
                _lane_halves(out[0].shape), out[2 * grp], out[2 * grp + 1]).astype(BF16)


def _swa_bias_tables():
    w = WINDOW
    row = lax.broadcasted_iota(jnp.int32, (w, 2 * w), 0)
    col = lax.broadcasted_iota(jnp.int32, (w, 2 * w), 1)
    slopes = 2.0 ** (-8.0 * jnp.arange(1, A_HEADS + 1, dtype=F32) / A_HEADS)
    tables = []
    for dist in (row - col, row + w - col):
        band = (dist >= 0) & (dist < w)
        alibi = -LOG2E * slopes[:, None, None] * dist.astype(F32)[None]
        tables.append(jnp.where(band[None], alibi, -jnp.inf))
    return jnp.stack(tables)


def _swa_attention(q, k, v_lo, v_hi, sinks):
    b, s, wq = q.shape
    tq = SWA_TQ
    bias = _swa_bias_tables()
    rows = pl.BlockSpec((1, tq, wq), lambda bi, qi: (bi, qi, 0))
    seq = pl.BlockSpec((1, s, LANES), lambda bi, qi: (bi, 0, 0))
    return pl.pallas_call(
        _swa_body,
        grid=(b, s // tq),
        in_specs=[pl.BlockSpec(memory_space=pltpu.SMEM), rows, seq, seq, seq, _resident(bias.shape)],
        out_specs=rows,
        out_shape=jax.ShapeDtypeStruct((b, s, wq), BF16),
        compiler_params=_params("parallel", "arbitrary"),
        name="swa_attention",
    )(sinks, q, k, v_lo, v_hi, bias)


def _swa_head_perm():
    n_groups = A_HEADS // 2
    idx = []
    for grp in range(n_groups):
        for head in (grp, grp + n_groups):
            idx.extend(range(head * A_HEAD_DIM, (head + 1) * A_HEAD_DIM))
    return np.asarray(idx, np.int32)


def _pad_cols(w, width):
    return jnp.pad(w, ((0, 0), (0, width - w.shape[1])))


def _even_weights(w_in, w_uq, w_ukv, w_out):
    perm = _swa_head_perm()
    nq = A_HEADS * A_HEAD_DIM
    nkv = A_KV_HEADS * A_HEAD_DIM
    o_cq = nq + 2 * nkv
    o_ckv = o_cq + B_Q_LORA
    o_kr = o_ckv + B_KV_LORA
    half = B_ROPE_DIM // 2
    aq = w_in[:, :nq][:, perm] * (A_HEAD_DIM ** -0.5)
    kr = w_in[:, o_kr:o_kr + B_ROPE_DIM]
    kr_swap = jnp.concatenate([kr[:, half:], kr[:, :half]], axis=1)
    place = lambda x: jnp.pad(x, ((0, 0), (B_NOPE_DIM, LANES - B_NOPE_DIM - B_ROPE_DIM)))
    win = jnp.concatenate([aq, w_in[:, nq:o_kr], place(kr), place(kr_swap)], axis=1).astype(BF16)

    dq = B_NOPE_DIM + B_ROPE_DIM
    uq = w_uq.reshape(B_Q_LORA, B_HEADS, dq)
    uq_swap = jnp.concatenate([jnp.zeros_like(uq[..., :B_NOPE_DIM]),
                               uq[..., B_NOPE_DIM + half:], uq[..., B_NOPE_DIM:B_NOPE_DIM + half]], axis=-1)
    pad_head = lambda x: jnp.pad(x, ((0, 0), (0, 0), (0, LANES - x.shape[-1]))).reshape(x.shape[0], B_HEADS * LANES)
    wuq = pad_head(uq).astype(BF16)
    wuqs = pad_head(uq_swap).astype(BF16)

    ukv = w_ukv.reshape(B_KV_LORA, B_HEADS, B_NOPE_DIM + B_V_DIM)
    wk = pad_head(ukv[..., :B_NOPE_DIM]).astype(BF16)
    wv = ukv[..., B_NOPE_DIM:].reshape(B_KV_LORA, B_HEADS * B_V_DIM).astype(BF16)

    wo_a = w_out[:nq][perm].astype(BF16)
    wo_b = w_out[nq:].astype(BF16)
    return win, wuq, wuqs, wk, wv, wo_a, wo_b


def _rope_tables(seq):
    inv = ROPE_THETA ** (-jnp.arange(0, B_ROPE_DIM, 2, dtype=F32) / B_ROPE_DIM)
    ang = jnp.arange(seq, dtype=F32)[:, None] * inv[None, :]
    cos, sin = jnp.cos(ang), jnp.sin(ang)
    ones = jnp.ones((seq, B_NOPE_DIM), F32)
    zeros = jnp.zeros((seq, B_NOPE_DIM), F32)
    tail = jnp.zeros((seq, LANES - B_NOPE_DIM - B_ROPE_DIM), F32)
    c_tab = jnp.concatenate([ones, cos, cos, tail], axis=1)
    s_tab = jnp.concatenate([zeros, -sin, sin, tail], axis=1)
    scale = (B_NOPE_DIM + B_ROPE_DIM) ** -0.5 * LOG2E
    return c_tab * scale, s_tab * scale, c_tab, s_tab


def _odd_weights(w_in, b_f):
    w = C_HEADS * C_HEAD_DIM
    win = jnp.concatenate([w_in[:, :3 * w], _pad_cols(w_in[:, 3 * w:], LANES)], axis=1).astype(BF16)
    bf = _pad_cols(b_f.reshape(1, C_HEADS).astype(F32), LANES)
    return win, bf


def kernel(x, p, ffa_norm, ffa_w_gate_up, ffa_w_down, mix_norm, ffb_norm, ffb_w_gate_up, ffb_w_down, ple_norm, ple_w_gate, ple_w_proj, ev_w_in, ev_sinks, ev_cq_norm, ev_w_uq, ev_ckv_norm, ev_w_ukv, ev_w_out, od_w_in, od_b_f, od_w_out, final_norm):
    batch, seq, d = x.shape
    depth = p.shape[0]
    t = batch * seq
    h = x.reshape(t, d)
    p_all = p.reshape(depth, t, PLE_DIM)
    vec = lambda a: a.reshape(1, -1).astype(F32)
    vecs = lambda a: a.reshape(a.shape[0], 1, a.shape[1]).astype(F32)
    tabs = _rope_tables(seq)
    ga, gm, gb, gp = vecs(ffa_norm), vecs(mix_norm), vecs(ffb_norm), vecs(ple_norm)
    wa_gu, wa_d = ffa_w_gate_up.astype(BF16), ffa_w_down.astype(BF16)
    wb_gu, wb_d = ffb_w_gate_up.astype(BF16), ffb_w_down.astype(BF16)
    wg, wp = ple_w_gate.astype(BF16), ple_w_proj.astype(BF16)
    gf = vec(final_norm)

    for i in range(depth):
        j = i // 2
        if i % 2 == 0:
            win, wuq, wuqs, wk, wv, wo_a, wo_b = _even_weights(ev_w_in[j], ev_w_uq[j], ev_w_ukv[j], ev_w_out[j])
            small = (win, vec(ev_cq_norm[j]), wuq, wuqs, vec(ev_ckv_norm[j]), wk, wv)
            h, aq, ak, av_lo, av_hi, mq, mk, mv_lo, mv_hi = _pre_even(h, i, ga, wa_gu, wa_d, gm, small, tabs, seq)
            b3 = lambda a: a.reshape(batch, seq, a.shape[-1])
            out_a = _swa_attention(b3(aq), b3(ak), b3(av_lo), b3(av_hi), ev_sinks[j].astype(F32) * LOG2E)
            out_b = _mla_attention(b3(mq), b3(mk), b3(mv_lo), b3(mv_hi))
            xs, wos = [out_a.reshape(t, -1), out_b.reshape(t, -1)], [wo_a, wo_b]
        else:
            win, bf = _odd_weights(od_w_in[j], od_b_f[j])
            h3, q_lo, q_hi, k, v_lo, v_hi, c_t = _pre_odd(h.reshape(batch, seq, d), i, ga, wa_gu, wa_d, gm, win, bf)
            h = h3.reshape(t, d)
            out = _fox_attention(q_lo, q_hi, k, v_lo, v_hi, c_t.reshape(batch, C_HEADS // 2, 2, seq))
            xs, wos = [out.reshape(t, -1)], [od_w_out[j].astype(BF16)]
        h = _post(h, xs, wos, i, gb, wb_gu, wb_d, p_all, gp, wg, wp, gf, final=(i == depth - 1))
    return h.reshape(batch, seq, d)
```

```python
import functools
import math

import numpy as np
import jax
import jax.numpy as jnp
from jax import lax
from jax.experimental import pallas as pl
from jax.experimental.pallas import tpu as pltpu

F32 = jnp.float32
BF16 = jnp.bfloat16

D_MODEL = 1024
D_FF = 2816
FFN_RES_SCALE = 0.5
RMS_EPS = 1e-6
PLE_DIM = 256
A_HEADS = 8
A_KV_HEADS = 2
A_HEAD_DIM = 64
WINDOW = 128
B_HEADS = 8
B_Q_LORA = 256
B_KV_LORA = 128
B_NOPE_DIM = 64
B_ROPE_DIM = 32
B_V_DIM = 64
ROPE_THETA = 10000.0
C_HEADS = 16
C_HEAD_DIM = 64

LOG2E = math.log2(math.e)
LANES = 128
HALF = LANES // 2
VMEM_LIMIT = 60 * 1024 * 1024

TOKEN_TILE = 512
FF_CHUNK = 256
ATT_TILE = 512
ATT_PAIRS = 2
SWA_TQ = 512


def _resident(shape):
    nd = len(shape)
    return pl.BlockSpec(shape, lambda *_: (0,) * nd, pipeline_mode=pl.Buffered(1))


def _layer_resident(stack, layer):
    _, a, b = stack.shape
    return pl.BlockSpec((None, a, b), lambda *_: (layer, 0, 0), pipeline_mode=pl.Buffered(1))


def _params(*sem):
    return pltpu.CompilerParams(dimension_semantics=sem, vmem_limit_bytes=VMEM_LIMIT)


def _rms(x, g):
    ms = jnp.mean(x * x, axis=-1, keepdims=True)
    return x * lax.rsqrt(ms + RMS_EPS) * g


def _dot(a, b):
    return jnp.dot(a, b, preferred_element_type=F32)


def _dot_nt(a, b):
    return lax.dot_general(a, b, (((1,), (1,)), ((), ())), preferred_element_type=F32)


def _lane_halves(shape):
    lane = lax.broadcasted_iota(jnp.int32, shape, len(shape) - 1)
    return lane < HALF


def _pair_halves(x, fill):
    lo = _lane_halves(x.shape)
    return jnp.where(lo, x, fill).astype(BF16), jnp.where(lo, fill, x).astype(BF16)


def _with_ones(v):
    return _pair_halves(v, 1.0)


def _ffn_half_step(x, g_ref, wgu_ref, wd_ref, act_ref):
    xn = _rms(x, g_ref[...]).astype(BF16)
    for c in range(D_FF // FF_CHUNK):
        lo = c * FF_CHUNK
        gate = _dot(xn, wgu_ref[:, lo:lo + FF_CHUNK])
        up = _dot(xn, wgu_ref[:, D_FF + lo:D_FF + lo + FF_CHUNK])
        act_ref[:, lo:lo + FF_CHUNK] = (gate * jax.nn.sigmoid(gate) * up).astype(BF16)
    return x + FFN_RES_SCALE * _dot(act_ref[...], wd_ref[...])


def _pre_even_body(h_ref, ga_ref, wgu_ref, wd_ref, gm_ref, win_ref, cqn_ref, wuq_ref, wuqs_ref,
                   ckvn_ref, wk_ref, wv_ref, cq_ref, sq_ref, ck_ref, sk_ref,
                   h1_ref, aq_ref, ak_ref, avlo_ref, avhi_ref, mq_ref, mk_ref, mvlo_ref, mvhi_ref, act_ref):
    y = _ffn_half_step(h_ref[...], ga_ref, wgu_ref, wd_ref, act_ref)
    h1_ref[...] = y
    xn = _rms(y, gm_ref[...]).astype(BF16)
    z = _dot(xn, win_ref[...])
    o_ak = A_HEADS * A_HEAD_DIM
    o_av = o_ak + LANES
    o_cq = o_av + LANES
    o_ckv = o_cq + B_Q_LORA
    o_kr = o_ckv + B_KV_LORA
    o_krs = o_kr + LANES
    aq_ref[...] = (z[:, :o_ak] * LOG2E).astype(BF16)
    ak_ref[...] = z[:, o_ak:o_av].astype(BF16)
    avlo_ref[...], avhi_ref[...] = _with_ones(z[:, o_av:o_cq])
    cqn = _rms(z[:, o_cq:o_ckv], cqn_ref[...]).astype(BF16)
    ckvn = _rms(z[:, o_ckv:o_kr], ckvn_ref[...]).astype(BF16)
    k_rope = z[:, o_kr:o_krs] * ck_ref[...] + z[:, o_krs:o_krs + LANES] * sk_ref[...]
    q_plain = _dot(cqn, wuq_ref[...])
    q_swap = _dot(cqn, wuqs_ref[...])
    k_nope = _dot(ckvn, wk_ref[...])
    cq = cq_ref[...]
    sq = sq_ref[...]
    for h in range(B_HEADS):
        sl = slice(h * LANES, (h + 1) * LANES)
        mq_ref[:, sl] = (q_plain[:, sl] * cq + q_swap[:, sl] * sq).astype(BF16)
        mk_ref[:, sl] = (k_nope[:, sl] + k_rope).astype(BF16)
    v = _dot(ckvn, wv_ref[...])
    for grp in range(B_HEADS // 2):
        sl = slice(grp * LANES, (grp + 1) * LANES)
        mvlo_ref[:, sl], mvhi_ref[:, sl] = _with_ones(v[:, sl])


def _pre_even(h, layer, ga, wgu, wd, gm, small, tabs, seq):
    t, d = h.shape
    tm = TOKEN_TILE
    n_pos = seq // tm
    row = lambda w: pl.BlockSpec((tm, w), lambda i: (i, 0))
    tab = pl.BlockSpec((tm, LANES), lambda i: (i % n_pos, 0))
    widths = (A_HEADS * A_HEAD_DIM, LANES, LANES, LANES, B_HEADS * LANES, B_HEADS * LANES,
              B_HEADS * B_V_DIM, B_HEADS * B_V_DIM)
    stacks = (ga, wgu, wd, gm)
    return pl.pallas_call(
        _pre_even_body,
        grid=(t // tm,),
        in_specs=[row(d)] + [_layer_resident(a, layer) for a in stacks]
        + [_resident(a.shape) for a in small] + [tab] * 4,
        out_specs=[row(d)] + [row(w) for w in widths],
        out_shape=[jax.ShapeDtypeStruct((t, d), F32)] + [jax.ShapeDtypeStruct((t, w), BF16) for w in widths],
        scratch_shapes=[pltpu.VMEM((tm, D_FF), BF16)],
        compiler_params=_params("parallel"),
        name="pre_even",
    )(h, *stacks, *small, *tabs)


def _pre_odd_body(h_ref, ga_ref, wgu_ref, wd_ref, gm_ref, win_ref, bf_ref,
                  h1_ref, qlo_ref, qhi_ref, k_ref, vlo_ref, vhi_ref, ct_ref, act_ref, carry_ref):
    @pl.when(pl.program_id(1) == 0)
    def _():
        carry_ref[...] = jnp.zeros_like(carry_ref)

    y = _ffn_half_step(h_ref[0], ga_ref, wgu_ref, wd_ref, act_ref)
    h1_ref[0] = y
    xn = _rms(y, gm_ref[...]).astype(BF16)
    w = C_HEADS * C_HEAD_DIM
    q = _dot(xn, win_ref[:, 0:w]) * (C_HEAD_DIM ** -0.5 * LOG2E)
    k_ref[0] = _dot(xn, win_ref[:, w:2 * w]).astype(BF16)
    v = _dot(xn, win_ref[:, 2 * w:3 * w])
    for grp in range(C_HEADS // 2):
        sl = slice(grp * LANES, (grp + 1) * LANES)
        qlo_ref[0, :, sl], qhi_ref[0, :, sl] = _pair_halves(q[:, sl], 0.0)
        vlo_ref[0, :, sl], vhi_ref[0, :, sl] = _with_ones(v[:, sl])
    f = _dot(xn, win_ref[:, 3 * w:3 * w + LANES]) + bf_ref[...]
    c = jnp.minimum(f, 0.0) - jnp.log1p(jnp.exp(-jnp.abs(f)))
    tm = c.shape[0]
    rows = lax.broadcasted_iota(jnp.int32, c.shape, 0)
    shift = 1
    while shift < tm:
        c = c + jnp.where(rows >= shift, pltpu.roll(c, shift, axis=0), 0.0)
        shift *= 2
    c = c + carry_ref[...]
    carry_ref[...] = c[tm - 1:tm, :]
    ct_ref[0] = jnp.transpose(c * LOG2E)[:C_HEADS, :]


def _pre_odd(h3, layer, ga, wgu, wd, gm, win, bf):
    b, s, d = h3.shape
    tm = TOKEN_TILE
    w = C_HEADS * C_HEAD_DIM
    blk = lambda n: pl.BlockSpec((1, tm, n), lambda bi, si: (bi, si, 0))
    stacks = (ga, wgu, wd, gm)
    return pl.pallas_call(
        _pre_odd_body,
        grid=(b, s // tm),
        in_specs=[blk(d)] + [_layer_resident(a, layer) for a in stacks] + [_resident(win.shape), _resident(bf.shape)],
        out_specs=[blk(d)] + [blk(w)] * 5 + [pl.BlockSpec((1, C_HEADS, tm), lambda bi, si: (bi, 0, si))],
        out_shape=[jax.ShapeDtypeStruct((b, s, d), F32)] + [jax.ShapeDtypeStruct((b, s, w), BF16)] * 5
        + [jax.ShapeDtypeStruct((b, C_HEADS, s), F32)],
        scratch_shapes=[pltpu.VMEM((tm, D_FF), BF16), pltpu.VMEM((1, LANES), F32)],
        compiler_params=_params("parallel", "arbitrary"),
        name="pre_odd",
    )(h3, *stacks, win, bf)


def _post_body(*refs, n_in, final):
    h_ref = refs[0]
    x_refs = refs[1:1 + n_in]
    wo_refs = refs[1 + n_in:1 + 2 * n_in]
    gb_ref, wgu_ref, wd_ref, p_ref, gp_ref, wg_ref, wp_ref, gf_ref, o_ref, act_ref = refs[1 + 2 * n_in:]
    y = h_ref[...]
    for x_ref, wo_ref in zip(x_refs, wo_refs):
        y = y + _dot(x_ref[...], wo_ref[...])
    embed = _dot(p_ref[...].astype(BF16), wp_ref[...])
    y = _ffn_half_step(y, gb_ref, wgu_ref, wd_ref, act_ref)
    gate = jax.nn.sigmoid(_dot(_rms(y, gp_ref[...]).astype(BF16), wg_ref[...]))
    y = y + gate * embed
    if final:
        y = _rms(y, gf_ref[...])
    o_ref[...] = y


def _post(h, xs, wos, layer, gb, wgu, wd, p_all, gp, wg, wp, gf, final):
    t, d = h.shape
    tm = TOKEN_TILE
    row = lambda w: pl.BlockSpec((tm, w), lambda i: (i, 0))
    return pl.pallas_call(
        functools.partial(_post_body, n_in=len(xs), final=final),
        grid=(t // tm,),
        in_specs=[row(d)] + [row(x.shape[1]) for x in xs] + [_resident(w.shape) for w in wos]
        + [_layer_resident(a, layer) for a in (gb, wgu, wd)]
        + [pl.BlockSpec((None, tm, PLE_DIM), lambda i: (layer, i, 0))]
        + [_layer_resident(a, layer) for a in (gp, wg, wp)] + [_resident(gf.shape)],
        out_specs=row(d),
        out_shape=jax.ShapeDtypeStruct((t, d), F32),
        scratch_shapes=[pltpu.VMEM((tm, D_FF), BF16)],
        compiler_params=_params("parallel"),
        name="post",
    )(h, *xs, *wos, gb, wgu, wd, p_all, gp, wg, wp, gf)


def _split_pair(q):
    lo = _lane_halves(q.shape)
    qf = q.astype(F32)
    return jnp.where(lo, qf, 0.0).astype(BF16), jnp.where(lo, 0.0, qf).astype(BF16)


def _lane_chunks(s):
    return [s[:, c * LANES:(c + 1) * LANES] for c in range(s.shape[1] // LANES)]


def _online_step(chunks, v_ones, state):
    m_prev, acc = state
    m_new = jnp.maximum(m_prev, jnp.max(functools.reduce(jnp.maximum, chunks), axis=-1, keepdims=True))
    alpha = jnp.exp2(m_prev - m_new)
    p = jnp.concatenate([jnp.exp2(c - m_new).astype(BF16) for c in chunks], axis=1)
    return m_new, alpha * acc + _dot(p, v_ones)


def _merge_pair(acc_lo, acc_hi):
    n0 = acc_lo / pltpu.roll(acc_lo, HALF, axis=1)
    n1 = acc_hi / pltpu.roll(acc_hi, HALF, axis=1)
    return jnp.where(_lane_halves(acc_lo.shape), n0, n1)


def _attend(blk, score_fn, v_fn, s_refs, m_ref, acc_ref):
    t = ATT_TILE
    half = t // 2
    heads = range(2 * ATT_PAIRS)
    m_ref[...] = jnp.full(m_ref.shape, -jnp.inf, F32)
    acc_ref[...] = jnp.zeros(acc_ref.shape, F32)

    def scores(j, s_ref):
        off = pl.multiple_of(j * t, t)
        for hh in heads:
            s_ref[hh] = score_fn(hh, 0, t, off, t)

    def consume(j, s_ref):
        off = pl.multiple_of(j * t, t)
        for hh in heads:
            m_ref[hh], acc_ref[hh] = _online_step(_lane_chunks(s_ref[hh]), v_fn(hh, off, t),
                                                  (m_ref[hh], acc_ref[hh]))

    row_halves = ((0, half), (half, t))

    def scores_diagonal(s_ref):
        off = pl.multiple_of(blk * t, t)
        for hh in heads:
            for row0, n_keys in row_halves:
                s_ref[hh, row0:row0 + half, 0:n_keys] = score_fn(hh, row0, half, off, n_keys)

    def consume_diagonal(s_ref):
        off = pl.multiple_of(blk * t, t)
        tri = (lax.broadcasted_iota(jnp.int32, (half, half), 1)
               <= lax.broadcasted_iota(jnp.int32, (half, half), 0))
        for hh in heads:
            for row0, n_keys in row_halves:
                rows = slice(row0, row0 + half)
                square = jnp.where(tri, s_ref[hh, rows, n_keys - half:n_keys], -jnp.inf)
                chunks = _lane_chunks(s_ref[hh, rows, 0:n_keys - half]) + _lane_chunks(square)
                m_ref[hh, rows], acc_ref[hh, rows] = _online_step(chunks, v_fn(hh, off, n_keys),
                                                                  (m_ref[hh, rows], acc_ref[hh, rows]))

    s_even, s_odd = s_refs
    scores(0, s_even)

    def two_tiles(pair, carry):
        j = 2 * pair
        scores(j + 1, s_odd)
        consume(j, s_even)
        scores(j + 2, s_even)
        consume(j + 1, s_odd)
        return carry

    lax.fori_loop(0, blk // 2, two_tiles, 0)

    @pl.when(blk % 2 == 1)
    def _():
        scores_diagonal(s_odd)
        consume(blk - 1, s_even)
        consume_diagonal(s_odd)

    @pl.when(blk % 2 == 0)
    def _():
        consume_diagonal(s_even)

    return [_merge_pair(acc_ref[2 * g], acc_ref[2 * g + 1]) for g in range(ATT_PAIRS)]


def _attend_scratch():
    t, n = ATT_TILE, 2 * ATT_PAIRS
    return [pltpu.VMEM((n, t, t), F32), pltpu.VMEM((n, t, t), F32),
            pltpu.VMEM((n, t, LANES), F32), pltpu.VMEM((n, t, LANES), F32)]


def _pair_values(vlo_ref, vhi_ref):
    def v_fn(hh, key0, n_keys):
        g, ref = hh // 2, (vlo_ref, vhi_ref)[hh % 2]
        return ref[0, pl.ds(key0, n_keys), g * LANES:(g + 1) * LANES]
    return v_fn


def _mla_body(q_ref, k_ref, vlo_ref, vhi_ref, o_ref, s0_ref, s1_ref, m_ref, acc_ref):
    t = ATT_TILE

    def block(blk, carry):
        base = pl.multiple_of(blk * t, t)

        def score_fn(hh, row0, n_rows, key0, n_keys):
            sl = slice(hh * LANES, (hh + 1) * LANES)
            rows = pl.ds(pl.multiple_of(base + row0, LANES), n_rows)
            return _dot_nt(q_ref[0, rows, sl], k_ref[0, pl.ds(key0, n_keys), sl])

        outs = _attend(blk, score_fn, _pair_values(vlo_ref, vhi_ref), (s0_ref, s1_ref), m_ref, acc_ref)
        for g, out in enumerate(outs):
            o_ref[0, pl.ds(base, t), g * LANES:(g + 1) * LANES] = out.astype(BF16)
        return carry

    lax.fori_loop(0, q_ref.shape[1] // t, block, 0)


def _mla_attention(q, k, v_lo, v_hi):
    b, s, _ = q.shape
    n_steps = B_HEADS // (2 * ATT_PAIRS)
    per_head = pl.BlockSpec((1, s, 2 * ATT_PAIRS * LANES), lambda bi, hg: (bi, 0, hg))
    per_pair = pl.BlockSpec((1, s, ATT_PAIRS * LANES), lambda bi, hg: (bi, 0, hg))
    return pl.pallas_call(
        _mla_body,
        grid=(b, n_steps),
        in_specs=[per_head, per_head, per_pair, per_pair],
        out_specs=per_pair,
        out_shape=jax.ShapeDtypeStruct((b, s, B_HEADS * B_V_DIM), BF16),
        scratch_shapes=_attend_scratch(),
        compiler_params=_params("parallel", "parallel"),
        name="mla_attention",
    )(q, k, v_lo, v_hi)


def _fox_body(qlo_ref, qhi_ref, k_ref, vlo_ref, vhi_ref, c_ref, o_ref, s0_ref, s1_ref, m_ref, acc_ref):
    t = ATT_TILE

    def block(blk, carry):
        base = pl.multiple_of(blk * t, t)
        rows = pl.ds(base, t)
        cqs = [jnp.transpose(jnp.broadcast_to(c_ref[0, hh // 2, hh % 2:hh % 2 + 1, rows], (LANES, t)))
               for hh in range(2 * ATT_PAIRS)]

        def score_fn(hh, row0, n_rows, key0, n_keys):
            g, q_ref = hh // 2, (qlo_ref, qhi_ref)[hh % 2]
            lanes = slice(g * LANES, (g + 1) * LANES)
            q = q_ref[0, pl.ds(pl.multiple_of(base + row0, LANES), n_rows), lanes]
            s = _dot_nt(q, k_ref[0, pl.ds(key0, n_keys), lanes])
            cq = cqs[hh][row0:row0 + n_rows]
            ck = c_ref[0, g, hh % 2:hh % 2 + 1, pl.ds(key0, n_keys)]
            return jnp.concatenate([c + cq - ck[:, n * LANES:(n + 1) * LANES]
                                    for n, c in enumerate(_lane_chunks(s))], axis=1)

        outs = _attend(blk, score_fn, _pair_values(vlo_ref, vhi_ref), (s0_ref, s1_ref), m_ref, acc_ref)
        for g, out in enumerate(outs):
            o_ref[0, rows, g * LANES:(g + 1) * LANES] = out.astype(BF16)
        return carry

    lax.fori_loop(0, k_ref.shape[1] // t, block, 0)


def _fox_attention(q_lo, q_hi, k, v_lo, v_hi, c_t):
    b, s, w = k.shape
    n_steps = C_HEADS // (2 * ATT_PAIRS)
    per_pair = pl.BlockSpec((1, s, ATT_PAIRS * LANES), lambda bi, hg: (bi, 0, hg))
    return pl.pallas_call(
        _fox_body,
        grid=(b, n_steps),
        in_specs=[per_pair] * 5 + [pl.BlockSpec((1, ATT_PAIRS, 2, s), lambda bi, hg: (bi, hg, 0, 0))],
        out_specs=per_pair,
        out_shape=jax.ShapeDtypeStruct((b, s, w), BF16),
        scratch_shapes=_attend_scratch(),
        compiler_params=_params("parallel", "parallel"),
        name="fox_attention",
    )(q_lo, q_hi, k, v_lo, v_hi, c_t)


def _swa_body(sink_ref, q_ref, k_ref, vlo_ref, vhi_ref, bias_ref, o_ref):
    w = WINDOW
    n_sub = SWA_TQ // w
    n_groups = A_HEADS // 2
    v_refs = (vlo_ref, vhi_ref)
    for sb in range(n_sub):
        blk = pl.program_id(1) * n_sub + sb
        k_start = pl.multiple_of(jnp.maximum(blk - 1, 0) * w, w)
        table = jnp.minimum(blk, 1)
        k_tile = k_ref[0, pl.ds(k_start, 2 * w), :]
        heads = [(grp, half) for grp in range(n_groups) for half in range(2)]
        scores = []
        for grp, half in heads:
            q_half = _split_pair(q_ref[0, sb * w:(sb + 1) * w, grp * LANES:(grp + 1) * LANES])[half]
            scores.append(_dot_nt(q_half, k_tile) + bias_ref[table, grp + n_groups * half])
        probs = []
        for (grp, half), s in zip(heads, scores):
            chunks = _lane_chunks(s)
            sink = sink_ref[grp + n_groups * half]
            row_max = jnp.max(functools.reduce(jnp.maximum, chunks), axis=-1, keepdims=True)
            m = jnp.maximum(jnp.broadcast_to(row_max, (w, LANES)), sink)
            e = jnp.concatenate([jnp.exp2(c - m).astype(BF16) for c in chunks], axis=1)
            probs.append((e, jnp.exp2(sink - m)))
        out = []
        for (grp, half), (e, sink_term) in zip(heads, probs):
            pv = _dot(e, v_refs[half][0, pl.ds(k_start, 2 * w), :])
            out.append(pv / (pltpu.roll(pv, HALF, axis=1) + sink_term))
        for grp in range(n_groups):
            o_ref[0, sb * w:(sb + 1) * w, grp * LANES:(grp + 1) * LANES] = jnp.where(
                _lane_halves(out[0].shape), out[2 * grp], out[2 * grp + 1]).astype(BF16)


def _swa_bias_tables():
    w = WINDOW
    row = lax.broadcasted_iota(jnp.int32, (w, 2 * w), 0)
    col = lax.broadcasted_iota(jnp.int32, (w, 2 * w), 1)
    slopes = 2.0 ** (-8.0 * jnp.arange(1, A_HEADS + 1, dtype=F32) / A_HEADS)
    tables = []
    for dist in (row - col, row + w - col):
        band = (dist >= 0) & (dist < w)
        alibi = -LOG2E * slopes[:, None, None] * dist.astype(F32)[None]
        tables.append(jnp.where(band[None], alibi, -jnp.inf))
    return jnp.stack(tables)


def _swa_attention(q, k, v_lo, v_hi, sinks):
    b, s, wq = q.shape
    tq = SWA_TQ
    bias = _swa_bias_tables()
    rows = pl.BlockSpec((1, tq, wq), lambda bi, qi: (bi, qi, 0))
    seq = pl.BlockSpec((1, s, LANES), lambda bi, qi: (bi, 0, 0))
    return pl.pallas_call(
        _swa_body,
        grid=(b, s // tq),
        in_specs=[pl.BlockSpec(memory_space=pltpu.SMEM), rows, seq, seq, seq, _resident(bias.shape)],
        out_specs=rows,
        out_shape=jax.ShapeDtypeStruct((b, s, wq), BF16),
        compiler_params=_params("parallel", "arbitrary"),
        name="swa_attention",
    )(sinks, q, k, v_lo, v_hi, bias)


def _swa_head_perm():
    n_groups = A_HEADS // 2
    idx = []
    for grp in range(n_groups):
        for head in (grp, grp + n_groups):
            idx.extend(range(head * A_HEAD_DIM, (head + 1) * A_HEAD_DIM))
    return np.asarray(idx, np.int32)


def _pad_cols(w, width):
    return jnp.pad(w, ((0, 0), (0, width - w.shape[1])))


def _even_weights(w_in, w_uq, w_ukv, w_out):
    perm = _swa_head_perm()
    nq = A_HEADS * A_HEAD_DIM
    nkv = A_KV_HEADS * A_HEAD_DIM
    o_cq = nq + 2 * nkv
    o_ckv = o_cq + B_Q_LORA
    o_kr = o_ckv + B_KV_LORA
    half = B_ROPE_DIM // 2
    aq = w_in[:, :nq][:, perm] * (A_HEAD_DIM ** -0.5)
    kr = w_in[:, o_kr:o_kr + B_ROPE_DIM]
    kr_swap = jnp.concatenate([kr[:, half:], kr[:, :half]], axis=1)
    place = lambda x: jnp.pad(x, ((0, 0), (B_NOPE_DIM, LANES - B_NOPE_DIM - B_ROPE_DIM)))
    win = jnp.concatenate([aq, w_in[:, nq:o_kr], place(kr), place(kr_swap)], axis=1).astype(BF16)

    dq = B_NOPE_DIM + B_ROPE_DIM
    uq = w_uq.reshape(B_Q_LORA, B_HEADS, dq)
    uq_swap = jnp.concatenate([jnp.zeros_like(uq[..., :B_NOPE_DIM]),
                               uq[..., B_NOPE_DIM + half:], uq[..., B_NOPE_DIM:B_NOPE_DIM + half]], axis=-1)
    pad_head = lambda x: jnp.pad(x, ((0, 0), (0, 0), (0, LANES - x.shape[-1]))).reshape(x.shape[0], B_HEADS * LANES)
    wuq = pad_head(uq).astype(BF16)
    wuqs = pad_head(uq_swap).astype(BF16)

    ukv = w_ukv.reshape(B_KV_LORA, B_HEADS, B_NOPE_DIM + B_V_DIM)
    wk = pad_head(ukv[..., :B_NOPE_DIM]).astype(BF16)
    wv = ukv[..., B_NOPE_DIM:].reshape(B_KV_LORA, B_HEADS * B_V_DIM).astype(BF16)

    wo_a = w_out[:nq][perm].astype(BF16)
    wo_b = w_out[nq:].astype(BF16)
    return win, wuq, wuqs, wk, wv, wo_a, wo_b


def _rope_tables(seq):
    inv = ROPE_THETA ** (-jnp.arange(0, B_ROPE_DIM, 2, dtype=F32) / B_ROPE_DIM)
    ang = jnp.arange(seq, dtype=F32)[:, None] * inv[None, :]
    cos, sin = jnp.cos(ang), jnp.sin(ang)
    ones = jnp.ones((seq, B_NOPE_DIM), F32)
    zeros = jnp.zeros((seq, B_NOPE_DIM), F32)
    tail = jnp.zeros((seq, LANES - B_NOPE_DIM - B_ROPE_DIM), F32)
    c_tab = jnp.concatenate([ones, cos, cos, tail], axis=1)
    s_tab = jnp.concatenate([zeros, -sin, sin, tail], axis=1)
    scale = (B_NOPE_DIM + B_ROPE_DIM) ** -0.5 * LOG2E
    return c_tab * scale, s_tab * scale, c_tab, s_tab


def _odd_weights(w_in, b_f):
    w = C_HEADS * C_HEAD_DIM
    win = jnp.concatenate([w_in[:, :3 * w], _pad_cols(w_in[:, 3 * w:], LANES)], axis=1).astype(BF16)
    bf = _pad_cols(b_f.reshape(1, C_HEADS).astype(F32), LANES)
    return win, bf


def kernel(x, p, ffa_norm, ffa_w_gate_up, ffa_w_down, mix_norm, ffb_norm, ffb_w_gate_up, ffb_w_down, ple_norm, ple_w_gate, ple_w_proj, ev_w_in, ev_sinks, ev_cq_norm, ev_w_uq, ev_ckv_norm, ev_w_ukv, ev_w_out, od_w_in, od_b_f, od_w_out, final_norm):
    batch, seq, d = x.shape
    depth = p.shape[0]
    t = batch * seq
    h = x.reshape(t, d)
    p_all = p.reshape(depth, t, PLE_DIM)
    vec = lambda a: a.reshape(1, -1).astype(F32)
    vecs = lambda a: a.reshape(a.shape[0], 1, a.shape[1]).astype(F32)
    tabs = _rope_tables(seq)
    ga, gm, gb, gp = vecs(ffa_norm), vecs(mix_norm), vecs(ffb_norm), vecs(ple_norm)
    wa_gu, wa_d = ffa_w_gate_up.astype(BF16), ffa_w_down.astype(BF16)
    wb_gu, wb_d = ffb_w_gate_up.astype(BF16), ffb_w_down.astype(BF16)
    wg, wp = ple_w_gate.astype(BF16), ple_w_proj.astype(BF16)
    gf = vec(final_norm)

    for i in range(depth):
        j = i // 2
        if i % 2 == 0:
            win, wuq, wuqs, wk, wv, wo_a, wo_b = _even_weights(ev_w_in[j], ev_w_uq[j], ev_w_ukv[j], ev_w_out[j])
            small = (win, vec(ev_cq_norm[j]), wuq, wuqs, vec(ev_ckv_norm[j]), wk, wv)
            h, aq, ak, av_lo, av_hi, mq, mk, mv_lo, mv_hi = _pre_even(h, i, ga, wa_gu, wa_d, gm, small, tabs, seq)
            b3 = lambda a: a.reshape(batch, seq, a.shape[-1])
            out_a = _swa_attention(b3(aq), b3(ak), b3(av_lo), b3(av_hi), ev_sinks[j].astype(F32) * LOG2E)
            out_b = _mla_attention(b3(mq), b3(mk), b3(mv_lo), b3(mv_hi))
            xs, wos = [out_a.reshape(t, -1), out_b.reshape(t, -1)], [wo_a, wo_b]
        else:
            win, bf = _odd_weights(od_w_in[j], od_b_f[j])
            h3, q_lo, q_hi, k, v_lo, v_hi, c_t = _pre_odd(h.reshape(batch, seq, d), i, ga, wa_gu, wa_d, gm, win, bf)
            h = h3.reshape(t, d)
            out = _fox_attention(q_lo, q_hi, k, v_lo, v_hi, c_t.reshape(batch, C_HEADS // 2, 2, seq))
            xs, wos = [out.reshape(t, -1)], [od_w_out[j].astype(BF16)]
        h = _post(h, xs, wos, i, gb, wb_gu, wb_d, p_all, gp, wg, wp, gf, final=(i == depth - 1))
    return h.reshape(batch, seq, d)
```

```python
import functools
import math

import numpy as np
import jax
import jax.numpy as jnp
from jax import lax
from jax.experimental import pallas as pl
from jax.experimental.pallas import tpu as pltpu

F32 = jnp.float32
BF16 = jnp.bfloat16

D_MODEL = 1024
D_FF = 2816
FFN_RES_SCALE = 0.5
RMS_EPS = 1e-6
PLE_DIM = 256
A_HEADS = 8
A_KV_HEADS = 2
A_HEAD_DIM = 64
WINDOW = 128
B_HEADS = 8
B_Q_LORA = 256
B_KV_LORA = 128
B_NOPE_DIM = 64
B_ROPE_DIM = 32
B_V_DIM = 64
ROPE_THETA = 10000.0
C_HEADS = 16
C_HEAD_DIM = 64

LOG2E = math.log2(math.e)
LANES = 128
HALF = LANES // 2
VMEM_LIMIT = 60 * 1024 * 1024

TOKEN_TILE = 512
FF_CHUNK = 256
ATT_TILE = 1024
ATT_PAIRS = 1
ATT_DIAG_STRIPS = 4
SWA_TQ = 512


def _resident(shape):
    nd = len(shape)
    return pl.BlockSpec(shape, lambda *_: (0,) * nd, pipeline_mode=pl.Buffered(1))


def _layer_resident(stack, layer):
    _, a, b = stack.shape
    return pl.BlockSpec((None, a, b), lambda *_: (layer, 0, 0), pipeline_mode=pl.Buffered(1))


def _params(*sem):
    return pltpu.CompilerParams(dimension_semantics=sem, vmem_limit_bytes=VMEM_LIMIT)


def _rms(x, g):
    ms = jnp.mean(x * x, axis=-1, keepdims=True)
    return x * lax.rsqrt(ms + RMS_EPS) * g


def _dot(a, b):
    return jnp.dot(a, b, preferred_element_type=F32)


def _dot_nt(a, b):
    return lax.dot_general(a, b, (((1,), (1,)), ((), ())), preferred_element_type=F32)


def _lane_halves(shape):
    lane = lax.broadcasted_iota(jnp.int32, shape, len(shape) - 1)
    return lane < HALF


def _pair_halves(x, fill):
    lo = _lane_halves(x.shape)
    return jnp.where(lo, x, fill).astype(BF16), jnp.where(lo, fill, x).astype(BF16)


def _with_ones(v):
    return _pair_halves(v, 1.0)


def _ffn_half_step(x, g_ref, wgu_ref, wd_ref, act_ref):
    xn = _rms(x, g_ref[...]).astype(BF16)
    for c in range(D_FF // FF_CHUNK):
        lo = c * FF_CHUNK
        gate = _dot(xn, wgu_ref[:, lo:lo + FF_CHUNK])
        up = _dot(xn, wgu_ref[:, D_FF + lo:D_FF + lo + FF_CHUNK])
        act_ref[:, lo:lo + FF_CHUNK] = (gate * jax.nn.sigmoid(gate) * up).astype(BF16)
    return x + FFN_RES_SCALE * _dot(act_ref[...], wd_ref[...])


def _pre_even_body(h_ref, ga_ref, wgu_ref, wd_ref, gm_ref, win_ref, cqn_ref, wuq_ref, wuqs_ref,
                   ckvn_ref, wk_ref, wv_ref, cq_ref, sq_ref, ck_ref, sk_ref,
                   h1_ref, aq_ref, ak_ref, avlo_ref, avhi_ref, mq_ref, mk_ref, mvlo_ref, mvhi_ref, act_ref):
    y = _ffn_half_step(h_ref[...], ga_ref, wgu_ref, wd_ref, act_ref)
    h1_ref[...] = y
    xn = _rms(y, gm_ref[...]).astype(BF16)
    z = _dot(xn, win_ref[...])
    o_ak = A_HEADS * A_HEAD_DIM
    o_av = o_ak + LANES
    o_cq = o_av + LANES
    o_ckv = o_cq + B_Q_LORA
    o_kr = o_ckv + B_KV_LORA
    o_krs = o_kr + LANES
    aq_ref[...] = (z[:, :o_ak] * LOG2E).astype(BF16)
    ak_ref[...] = z[:, o_ak:o_av].astype(BF16)
    avlo_ref[...], avhi_ref[...] = _with_ones(z[:, o_av:o_cq])
    cqn = _rms(z[:, o_cq:o_ckv], cqn_ref[...]).astype(BF16)
    ckvn = _rms(z[:, o_ckv:o_kr], ckvn_ref[...]).astype(BF16)
    k_rope = z[:, o_kr:o_krs] * ck_ref[...] + z[:, o_krs:o_krs + LANES] * sk_ref[...]
    q_plain = _dot(cqn, wuq_ref[...])
    q_swap = _dot(cqn, wuqs_ref[...])
    k_nope = _dot(ckvn, wk_ref[...])
    cq = cq_ref[...]
    sq = sq_ref[...]
    for h in range(B_HEADS):
        sl = slice(h * LANES, (h + 1) * LANES)
        mq_ref[:, sl] = (q_plain[:, sl] * cq + q_swap[:, sl] * sq).astype(BF16)
        mk_ref[:, sl] = (k_nope[:, sl] + k_rope).astype(BF16)
    v = _dot(ckvn, wv_ref[...])
    for grp in range(B_HEADS // 2):
        sl = slice(grp * LANES, (grp + 1) * LANES)
        mvlo_ref[:, sl], mvhi_ref[:, sl] = _with_ones(v[:, sl])


def _pre_even(h, layer, ga, wgu, wd, gm, small, tabs, seq):
    t, d = h.shape
    tm = TOKEN_TILE
    n_pos = seq // tm
    row = lambda w: pl.BlockSpec((tm, w), lambda i: (i, 0))
    tab = pl.BlockSpec((tm, LANES), lambda i: (i % n_pos, 0))
    widths = (A_HEADS * A_HEAD_DIM, LANES, LANES, LANES, B_HEADS * LANES, B_HEADS * LANES,
              B_HEADS * B_V_DIM, B_HEADS * B_V_DIM)
    stacks = (ga, wgu, wd, gm)
    return pl.pallas_call(
        _pre_even_body,
        grid=(t // tm,),
        in_specs=[row(d)] + [_layer_resident(a, layer) for a in stacks]
        + [_resident(a.shape) for a in small] + [tab] * 4,
        out_specs=[row(d)] + [row(w) for w in widths],
        out_shape=[jax.ShapeDtypeStruct((t, d), F32)] + [jax.ShapeDtypeStruct((t, w), BF16) for w in widths],
        scratch_shapes=[pltpu.VMEM((tm, D_FF), BF16)],
        compiler_params=_params("parallel"),
        name="pre_even",
    )(h, *stacks, *small, *tabs)


def _pre_odd_body(h_ref, ga_ref, wgu_ref, wd_ref, gm_ref, win_ref, bf_ref,
                  h1_ref, qlo_ref, qhi_ref, k_ref, vlo_ref, vhi_ref, ct_ref, act_ref, carry_ref):
    @pl.when(pl.program_id(1) == 0)
    def _():
        carry_ref[...] = jnp.zeros_like(carry_ref)

    y = _ffn_half_step(h_ref[0], ga_ref, wgu_ref, wd_ref, act_ref)
    h1_ref[0] = y
    xn = _rms(y, gm_ref[...]).astype(BF16)
    w = C_HEADS * C_HEAD_DIM
    q = _dot(xn, win_ref[:, 0:w]) * (C_HEAD_DIM ** -0.5 * LOG2E)
    k_ref[0] = _dot(xn, win_ref[:, w:2 * w]).astype(BF16)
    v = _dot(xn, win_ref[:, 2 * w:3 * w])
    for grp in range(C_HEADS // 2):
        sl = slice(grp * LANES, (grp + 1) * LANES)
        qlo_ref[0, :, sl], qhi_ref[0, :, sl] = _pair_halves(q[:, sl], 0.0)
        vlo_ref[0, :, sl], vhi_ref[0, :, sl] = _with_ones(v[:, sl])
    f = _dot(xn, win_ref[:, 3 * w:3 * w + LANES]) + bf_ref[...]
    c = jnp.minimum(f, 0.0) - jnp.log1p(jnp.exp(-jnp.abs(f)))
    tm = c.shape[0]
    rows = lax.broadcasted_iota(jnp.int32, c.shape, 0)
    shift = 1
    while shift < tm:
        c = c + jnp.where(rows >= shift, pltpu.roll(c, shift, axis=0), 0.0)
        shift *= 2
    c = c + carry_ref[...]
    carry_ref[...] = c[tm - 1:tm, :]
    ct_ref[0] = jnp.transpose(c * LOG2E)[:C_HEADS, :]


def _pre_odd(h3, layer, ga, wgu, wd, gm, win, bf):
    b, s, d = h3.shape
    tm = TOKEN_TILE
    w = C_HEADS * C_HEAD_DIM
    blk = lambda n: pl.BlockSpec((1, tm, n), lambda bi, si: (bi, si, 0))
    stacks = (ga, wgu, wd, gm)
    return pl.pallas_call(
        _pre_odd_body,
        grid=(b, s // tm),
        in_specs=[blk(d)] + [_layer_resident(a, layer) for a in stacks] + [_resident(win.shape), _resident(bf.shape)],
        out_specs=[blk(d)] + [blk(w)] * 5 + [pl.BlockSpec((1, C_HEADS, tm), lambda bi, si: (bi, 0, si))],
        out_shape=[jax.ShapeDtypeStruct((b, s, d), F32)] + [jax.ShapeDtypeStruct((b, s, w), BF16)] * 5
        + [jax.ShapeDtypeStruct((b, C_HEADS, s), F32)],
        scratch_shapes=[pltpu.VMEM((tm, D_FF), BF16), pltpu.VMEM((1, LANES), F32)],
        compiler_params=_params("parallel", "arbitrary"),
        name="pre_odd",
    )(h3, *stacks, win, bf)


def _post_body(*refs, n_in, final):
    h_ref = refs[0]
    x_refs = refs[1:1 + n_in]
    wo_refs = refs[1 + n_in:1 + 2 * n_in]
    gb_ref, wgu_ref, wd_ref, p_ref, gp_ref, wg_ref, wp_ref, gf_ref, o_ref, act_ref = refs[1 + 2 * n_in:]
    y = h_ref[...]
    for x_ref, wo_ref in zip(x_refs, wo_refs):
        y = y + _dot(x_ref[...], wo_ref[...])
    y = _ffn_half_step(y, gb_ref, wgu_ref, wd_ref, act_ref)
    gate = jax.nn.sigmoid(_dot(_rms(y, gp_ref[...]).astype(BF16), wg_ref[...]))
    y = y + gate * _dot(p_ref[...].astype(BF16), wp_ref[...])
    if final:
        y = _rms(y, gf_ref[...])
    o_ref[...] = y


def _post(h, xs, wos, layer, gb, wgu, wd, p_all, gp, wg, wp, gf, final):
    t, d = h.shape
    tm = TOKEN_TILE
    row = lambda w: pl.BlockSpec((tm, w), lambda i: (i, 0))
    return pl.pallas_call(
        functools.partial(_post_body, n_in=len(xs), final=final),
        grid=(t // tm,),
        in_specs=[row(d)] + [row(x.shape[1]) for x in xs] + [_resident(w.shape) for w in wos]
        + [_layer_resident(a, layer) for a in (gb, wgu, wd)]
        + [pl.BlockSpec((None, tm, PLE_DIM), lambda i: (layer, i, 0))]
        + [_layer_resident(a, layer) for a in (gp, wg, wp)] + [_resident(gf.shape)],
        out_specs=row(d),
        out_shape=jax.ShapeDtypeStruct((t, d), F32),
        scratch_shapes=[pltpu.VMEM((tm, D_FF), BF16)],
        compiler_params=_params("parallel"),
        name="post",
    )(h, *xs, *wos, gb, wgu, wd, p_all, gp, wg, wp, gf)


def _split_pair(q):
    lo = _lane_halves(q.shape)
    qf = q.astype(F32)
    return jnp.where(lo, qf, 0.0).astype(BF16), jnp.where(lo, 0.0, qf).astype(BF16)


def _lane_chunks(s):
    return [s[:, c * LANES:(c + 1) * LANES] for c in range(s.shape[1] // LANES)]


def _online_step(chunks, v_ones, state):
    m_prev, acc = state
    m_new = jnp.maximum(m_prev, jnp.max(functools.reduce(jnp.maximum, chunks), axis=-1, keepdims=True))
    alpha = jnp.exp2(m_prev - m_new)
    p = jnp.concatenate([jnp.exp2(c - m_new).astype(BF16) for c in chunks], axis=1)
    return m_new, alpha * acc + _dot(p, v_ones)


def _merge_pair(acc_lo, acc_hi):
    n0 = acc_lo / pltpu.roll(acc_lo, HALF, axis=1)
    n1 = acc_hi / pltpu.roll(acc_hi, HALF, axis=1)
    return jnp.where(_lane_halves(acc_lo.shape), n0, n1)


def _attend(blk, score_fn, v_fn, s_refs, m_ref, acc_ref):
    t = ATT_TILE
    heads = range(2 * ATT_PAIRS)
    m_ref[...] = jnp.full(m_ref.shape, -jnp.inf, F32)
    acc_ref[...] = jnp.zeros(acc_ref.shape, F32)

    def scores(j, s_ref):
        off = pl.multiple_of(j * t, t)
        for hh in heads:
            s_ref[hh] = score_fn(hh, 0, t, off, t)

    def consume(j, s_ref):
        off = pl.multiple_of(j * t, t)
        for hh in heads:
            m_ref[hh], acc_ref[hh] = _online_step(_lane_chunks(s_ref[hh]), v_fn(hh, off, t),
                                                  (m_ref[hh], acc_ref[hh]))

    strip = t // ATT_DIAG_STRIPS
    row_strips = tuple((r * strip, (r + 1) * strip) for r in range(ATT_DIAG_STRIPS))

    def scores_diagonal(s_ref):
        off = pl.multiple_of(blk * t, t)
        for hh in heads:
            for row0, n_keys in row_strips:
                s_ref[hh, row0:row0 + strip, 0:n_keys] = score_fn(hh, row0, strip, off, n_keys)

    def consume_diagonal(s_ref):
        off = pl.multiple_of(blk * t, t)
        tri = (lax.broadcasted_iota(jnp.int32, (strip, strip), 1)
               <= lax.broadcasted_iota(jnp.int32, (strip, strip), 0))
        for hh in heads:
            for row0, n_keys in row_strips:
                rows = slice(row0, row0 + strip)
                square = jnp.where(tri, s_ref[hh, rows, n_keys - strip:n_keys], -jnp.inf)
                chunks = _lane_chunks(s_ref[hh, rows, 0:n_keys - strip]) + _lane_chunks(square)
                m_ref[hh, rows], acc_ref[hh, rows] = _online_step(chunks, v_fn(hh, off, n_keys),
                                                                  (m_ref[hh, rows], acc_ref[hh, rows]))

    s_even, s_odd = s_refs
    scores(0, s_even)

    def two_tiles(pair, carry):
        j = 2 * pair
        scores(j + 1, s_odd)
        consume(j, s_even)
        scores(j + 2, s_even)
        consume(j + 1, s_odd)
        return carry

    lax.fori_loop(0, blk // 2, two_tiles, 0)

    @pl.when(blk % 2 == 1)
    def _():
        scores_diagonal(s_odd)
        consume(blk - 1, s_even)
        consume_diagonal(s_odd)

    @pl.when(blk % 2 == 0)
    def _():
        consume_diagonal(s_even)

    return [_merge_pair(acc_ref[2 * g], acc_ref[2 * g + 1]) for g in range(ATT_PAIRS)]


def _attend_scratch():
    t, n = ATT_TILE, 2 * ATT_PAIRS
    return [pltpu.VMEM((n, t, t), F32), pltpu.VMEM((n, t, t), F32),
            pltpu.VMEM((n, t, LANES), F32), pltpu.VMEM((n, t, LANES), F32)]


def _pair_values(vlo_ref, vhi_ref):
    def v_fn(hh, key0, n_keys):
        g, ref = hh // 2, (vlo_ref, vhi_ref)[hh % 2]
        return ref[0, pl.ds(key0, n_keys), g * LANES:(g + 1) * LANES]
    return v_fn


def _mla_body(q_ref, k_ref, vlo_ref, vhi_ref, o_ref, s0_ref, s1_ref, m_ref, acc_ref):
    t = ATT_TILE

    def block(blk, carry):
        base = pl.multiple_of(blk * t, t)

        def score_fn(hh, row0, n_rows, key0, n_keys):
            sl = slice(hh * LANES, (hh + 1) * LANES)
            rows = pl.ds(pl.multiple_of(base + row0, LANES), n_rows)
            return _dot_nt(q_ref[0, rows, sl], k_ref[0, pl.ds(key0, n_keys), sl])

        outs = _attend(blk, score_fn, _pair_values(vlo_ref, vhi_ref), (s0_ref, s1_ref), m_ref, acc_ref)
        for g, out in enumerate(outs):
            o_ref[0, pl.ds(base, t), g * LANES:(g + 1) * LANES] = out.astype(BF16)
        return carry

    lax.fori_loop(0, q_ref.shape[1] // t, block, 0)


def _mla_attention(q, k, v_lo, v_hi):
    b, s, _ = q.shape
    n_steps = B_HEADS // (2 * ATT_PAIRS)
    per_head = pl.BlockSpec((1, s, 2 * ATT_PAIRS * LANES), lambda bi, hg: (bi, 0, hg))
    per_pair = pl.BlockSpec((1, s, ATT_PAIRS * LANES), lambda bi, hg: (bi, 0, hg))
    return pl.pallas_call(
        _mla_body,
        grid=(b, n_steps),
        in_specs=[per_head, per_head, per_pair, per_pair],
        out_specs=per_pair,
        out_shape=jax.ShapeDtypeStruct((b, s, B_HEADS * B_V_DIM), BF16),
        scratch_shapes=_attend_scratch(),
        compiler_params=_params("parallel", "parallel"),
        name="mla_attention",
    )(q, k, v_lo, v_hi)


def _fox_body(qlo_ref, qhi_ref, k_ref, vlo_ref, vhi_ref, c_ref, o_ref, s0_ref, s1_ref, m_ref, acc_ref):
    t = ATT_TILE

    def block(blk, carry):
        base = pl.multiple_of(blk * t, t)
        rows = pl.ds(base, t)
        cqs = [jnp.transpose(jnp.broadcast_to(c_ref[0, hh // 2, hh % 2:hh % 2 + 1, rows], (LANES, t)))
               for hh in range(2 * ATT_PAIRS)]

        def score_fn(hh, row0, n_rows, key0, n_keys):
            g, q_ref = hh // 2, (qlo_ref, qhi_ref)[hh % 2]
            lanes = slice(g * LANES, (g + 1) * LANES)
            q = q_ref[0, pl.ds(pl.multiple_of(base + row0, LANES), n_rows), lanes]
            s = _dot_nt(q, k_ref[0, pl.ds(key0, n_keys), lanes])
            cq = cqs[hh][row0:row0 + n_rows]
            ck = c_ref[0, g, hh % 2:hh % 2 + 1, pl.ds(key0, n_keys)]
            return jnp.concatenate([c + cq - ck[:, n * LANES:(n + 1) * LANES]
                                    for n, c in enumerate(_lane_chunks(s))], axis=1)

        outs = _attend(blk, score_fn, _pair_values(vlo_ref, vhi_ref), (s0_ref, s1_ref), m_ref, acc_ref)
        for g, out in enumerate(outs):
            o_ref[0, rows, g * LANES:(g + 1) * LANES] = out.astype(BF16)
        return carry

    lax.fori_loop(0, k_ref.shape[1] // t, block, 0)


def _fox_attention(q_lo, q_hi, k, v_lo, v_hi, c_t):
    b, s, w = k.shape
    n_steps = C_HEADS // (2 * ATT_PAIRS)
    per_pair = pl.BlockSpec((1, s, ATT_PAIRS * LANES), lambda bi, hg: (bi, 0, hg))
    return pl.pallas_call(
        _fox_body,
        grid=(b, n_steps),
        in_specs=[per_pair] * 5 + [pl.BlockSpec((1, ATT_PAIRS, 2, s), lambda bi, hg: (bi, hg, 0, 0))],
        out_specs=per_pair,
        out_shape=jax.ShapeDtypeStruct((b, s, w), BF16),
        scratch_shapes=_attend_scratch(),
        compiler_params=_params("parallel", "parallel"),
        name="fox_attention",
    )(q_lo, q_hi, k, v_lo, v_hi, c_t)


def _swa_body(sink_ref, q_ref, k_ref, vlo_ref, vhi_ref, bias_ref, o_ref):
    w = WINDOW
    n_sub = SWA_TQ // w
    n_groups = A_HEADS // 2
    v_refs = (vlo_ref, vhi_ref)
    for sb in range(n_sub):
        blk = pl.program_id(1) * n_sub + sb
        k_start = pl.multiple_of(jnp.maximum(blk - 1, 0) * w, w)
        table = jnp.minimum(blk, 1)
        k_tile = k_ref[0, pl.ds(k_start, 2 * w), :]
        heads = [(grp, half) for grp in range(n_groups) for half in range(2)]
        scores = []
        for grp, half in heads:
            q_half = _split_pair(q_ref[0, sb * w:(sb + 1) * w, grp * LANES:(grp + 1) * LANES])[half]
            scores.append(_dot_nt(q_half, k_tile) + bias_ref[table, grp + n_groups * half])
        probs = []
        for (grp, half), s in zip(heads, scores):
            chunks = _lane_chunks(s)
            sink = sink_ref[grp + n_groups * half]
            row_max = jnp.max(functools.reduce(jnp.maximum, chunks), axis=-1, keepdims=True)
            m = jnp.maximum(jnp.broadcast_to(row_max, (w, LANES)), sink)
            e = jnp.concatenate([jnp.exp2(c - m).astype(BF16) for c in chunks], axis=1)
            probs.append((e, jnp.exp2(sink - m)))
        out = []
        for (grp, half), (e, sink_term) in zip(heads, probs):
            pv = _dot(e, v_refs[half][0, pl.ds(k_start, 2 * w), :])
            out.append(pv / (pltpu.roll(pv, HALF, axis=1) + sink_term))
        for grp in range(n_groups):
            o_ref[0, sb * w:(sb + 1) * w, grp * LANES:(grp + 1) * LANES] = jnp.where(
                _lane_halves(out[0].shape), out[2 * grp], out[2 * grp + 1]).astype(BF16)


def _swa_bias_tables():
    w = WINDOW
    row = lax.broadcasted_iota(jnp.int32, (w, 2 * w), 0)
    col = lax.broadcasted_iota(jnp.int32, (w, 2 * w), 1)
    slopes = 2.0 ** (-8.0 * jnp.arange(1, A_HEADS + 1, dtype=F32) / A_HEADS)
    tables = []
    for dist in (row - col, row + w - col):
        band = (dist >= 0) & (dist < w)
        alibi = -LOG2E * slopes[:, None, None] * dist.astype(F32)[None]
        tables.append(jnp.where(band[None], alibi, -jnp.inf))
    return jnp.stack(tables)


def _swa_attention(q, k, v_lo, v_hi, sinks):
    b, s, wq = q.shape
    tq = SWA_TQ
    bias = _swa_bias_tables()
    rows = pl.BlockSpec((1, tq, wq), lambda bi, qi: (bi, qi, 0))
    seq = pl.BlockSpec((1, s, LANES), lambda bi, qi: (bi, 0, 0))
    return pl.pallas_call(
        _swa_body,
        grid=(b, s // tq),
        in_specs=[pl.BlockSpec(memory_space=pltpu.SMEM), rows, seq, seq, seq, _resident(bias.shape)],
        out_specs=rows,
        out_shape=jax.ShapeDtypeStruct((b, s, wq), BF16),
        compiler_params=_params("parallel", "arbitrary"),
        name="swa_attention",
    )(sinks, q, k, v_lo, v_hi, bias)


def _swa_head_perm():
    n_groups = A_HEADS // 2
    idx = []
    for grp in range(n_groups):
        for head in (grp, grp + n_groups):
            idx.extend(range(head * A_HEAD_DIM, (head + 1) * A_HEAD_DIM))
    return np.asarray(idx, np.int32)


def _pad_cols(w, width):
    return jnp.pad(w, ((0, 0), (0, width - w.shape[1])))


def _even_weights(w_in, w_uq, w_ukv, w_out):
    perm = _swa_head_perm()
    nq = A_HEADS * A_HEAD_DIM
    nkv = A_KV_HEADS * A_HEAD_DIM
    o_cq = nq + 2 * nkv
    o_ckv = o_cq + B_Q_LORA
    o_kr = o_ckv + B_KV_LORA
    half = B_ROPE_DIM // 2
    aq = w_in[:, :nq][:, perm] * (A_HEAD_DIM ** -0.5)
    kr = w_in[:, o_kr:o_kr + B_ROPE_DIM]
    kr_swap = jnp.concatenate([kr[:, half:], kr[:, :half]], axis=1)
    place = lambda x: jnp.pad(x, ((0, 0), (B_NOPE_DIM, LANES - B_NOPE_DIM - B_ROPE_DIM)))
    win = jnp.concatenate([aq, w_in[:, nq:o_kr], place(kr), place(kr_swap)], axis=1).astype(BF16)

    dq = B_NOPE_DIM + B_ROPE_DIM
    uq = w_uq.reshape(B_Q_LORA, B_HEADS, dq)
    uq_swap = jnp.concatenate([jnp.zeros_like(uq[..., :B_NOPE_DIM]),
                               uq[..., B_NOPE_DIM + half:], uq[..., B_NOPE_DIM:B_NOPE_DIM + half]], axis=-1)
    pad_head = lambda x: jnp.pad(x, ((0, 0), (0, 0), (0, LANES - x.shape[-1]))).reshape(x.shape[0], B_HEADS * LANES)
    wuq = pad_head(uq).astype(BF16)
    wuqs = pad_head(uq_swap).astype(BF16)

    ukv = w_ukv.reshape(B_KV_LORA, B_HEADS, B_NOPE_DIM + B_V_DIM)
    wk = pad_head(ukv[..., :B_NOPE_DIM]).astype(BF16)
    wv = ukv[..., B_NOPE_DIM:].reshape(B_KV_LORA, B_HEADS * B_V_DIM).astype(BF16)

    wo_a = w_out[:nq][perm].astype(BF16)
    wo_b = w_out[nq:].astype(BF16)
    return win, wuq, wuqs, wk, wv, wo_a, wo_b


def _rope_tables(seq):
    inv = ROPE_THETA ** (-jnp.arange(0, B_ROPE_DIM, 2, dtype=F32) / B_ROPE_DIM)
    ang = jnp.arange(seq, dtype=F32)[:, None] * inv[None, :]
    cos, sin = jnp.cos(ang), jnp.sin(ang)
    ones = jnp.ones((seq, B_NOPE_DIM), F32)
    zeros = jnp.zeros((seq, B_NOPE_DIM), F32)
    tail = jnp.zeros((seq, LANES - B_NOPE_DIM - B_ROPE_DIM), F32)
    c_tab = jnp.concatenate([ones, cos, cos, tail], axis=1)
    s_tab = jnp.concatenate([zeros, -sin, sin, tail], axis=1)
    scale = (B_NOPE_DIM + B_ROPE_DIM) ** -0.5 * LOG2E
    return c_tab * scale, s_tab * scale, c_tab, s_tab


def _odd_weights(w_in, b_f):
    w = C_HEADS * C_HEAD_DIM
    win = jnp.concatenate([w_in[:, :3 * w], _pad_cols(w_in[:, 3 * w:], LANES)], axis=1).astype(BF16)
    bf = _pad_cols(b_f.reshape(1, C_HEADS).astype(F32), LANES)
    return win, bf


def kernel(x, p, ffa_norm, ffa_w_gate_up, ffa_w_down, mix_norm, ffb_norm, ffb_w_gate_up, ffb_w_down, ple_norm, ple_w_gate, ple_w_proj, ev_w_in, ev_sinks, ev_cq_norm, ev_w_uq, ev_ckv_norm, ev_w_ukv, ev_w_out, od_w_in, od_b_f, od_w_out, final_norm):
    batch, seq, d = x.shape
    depth = p.shape[0]
    t = batch * seq
    h = x.reshape(t, d)
    p_all = p.reshape(depth, t, PLE_DIM)
    vec = lambda a: a.reshape(1, -1).astype(F32)
    vecs = lambda a: a.reshape(a.shape[0], 1, a.shape[1]).astype(F32)
    tabs = _rope_tables(seq)
    ga, gm, gb, gp = vecs(ffa_norm), vecs(mix_norm), vecs(ffb_norm), vecs(ple_norm)
    wa_gu, wa_d = ffa_w_gate_up.astype(BF16), ffa_w_down.astype(BF16)
    wb_gu, wb_d = ffb_w_gate_up.astype(BF16), ffb_w_down.astype(BF16)
    wg, wp = ple_w_gate.astype(BF16), ple_w_proj.astype(BF16)
    gf = vec(final_norm)

    for i in range(depth):
        j = i // 2
        if i % 2 == 0:
            win, wuq, wuqs, wk, wv, wo_a, wo_b = _even_weights(ev_w_in[j], ev_w_uq[j], ev_w_ukv[j], ev_w_out[j])
            small = (win, vec(ev_cq_norm[j]), wuq, wuqs, vec(ev_ckv_norm[j]), wk, wv)
            h, aq, ak, av_lo, av_hi, mq, mk, mv_lo, mv_hi = _pre_even(h, i, ga, wa_gu, wa_d, gm, small, tabs, seq)
            b3 = lambda a: a.reshape(batch, seq, a.shape[-1])
            out_a = _swa_attention(b3(aq), b3(ak), b3(av_lo), b3(av_hi), ev_sinks[j].astype(F32) * LOG2E)
            out_b = _mla_attention(b3(mq), b3(mk), b3(mv_lo), b3(mv_hi))
            xs, wos = [out_a.reshape(t, -1), out_b.reshape(t, -1)], [wo_a, wo_b]
        else:
            win, bf = _odd_weights(od_w_in[j], od_b_f[j])
            h3, q_lo, q_hi, k, v_lo, v_hi, c_t = _pre_odd(h.reshape(batch, seq, d), i, ga, wa_gu, wa_d, gm, win, bf)
            h = h3.reshape(t, d)
            out = _fox_attention(q_lo, q_hi, k, v_lo, v_hi, c_t.reshape(batch, C_HEADS // 2, 2, seq))
            xs, wos = [out.reshape(t, -1)], [od_w_out[j].astype(BF16)]
        h = _post(h, xs, wos, i, gb, wb_gu, wb_d, p_all, gp, wg, wp, gf, final=(i == depth - 1))
    return h.reshape(batch, seq, d)
```

```python
import functools
import math

import numpy as np
import jax
import jax.numpy as jnp
from jax import lax
from jax.experimental import pallas as pl
from jax.experimental.pallas import tpu as pltpu

F32 = jnp.float32
BF16 = jnp.bfloat16

D_MODEL = 1024
D_FF = 2816
FFN_RES_SCALE = 0.5
RMS_EPS = 1e-6
PLE_DIM = 256
A_HEADS = 8
A_KV_HEADS = 2
A_HEAD_DIM = 64
WINDOW = 128
B_HEADS = 8
B_Q_LORA = 256
B_KV_LORA = 128
B_NOPE_DIM = 64
B_ROPE_DIM = 32
B_V_DIM = 64
ROPE_THETA = 10000.0
C_HEADS = 16
C_HEAD_DIM = 64

LOG2E = math.log2(math.e)
LANES = 128
HALF = LANES // 2
VMEM_LIMIT = 64 * 1024 * 1024

TOKEN_TILE = 512
FF_CHUNK = 256
ATT_TILE = 1024
MLA_PAIRS = 1
FOX_PAIRS = 2
ATT_DIAG_STRIPS = 4
SWA_TQ = 512


def _resident(shape):
    nd = len(shape)
    return pl.BlockSpec(shape, lambda *_: (0,) * nd, pipeline_mode=pl.Buffered(1))


def _layer_resident(stack, layer):
    _, a, b = stack.shape
    return pl.BlockSpec((None, a, b), lambda *_: (layer, 0, 0), pipeline_mode=pl.Buffered(1))


def _params(*sem):
    return pltpu.CompilerParams(dimension_semantics=sem, vmem_limit_bytes=VMEM_LIMIT)


def _rms(x, g):
    ms = jnp.mean(x * x, axis=-1, keepdims=True)
    return x * lax.rsqrt(ms + RMS_EPS) * g


def _dot(a, b):
    return jnp.dot(a, b, preferred_element_type=F32)


def _dot_nt(a, b):
    return lax.dot_general(a, b, (((1,), (1,)), ((), ())), preferred_element_type=F32)


def _lane_halves(shape):
    lane = lax.broadcasted_iota(jnp.int32, shape, len(shape) - 1)
    return lane < HALF


def _pair_halves(x, fill):
    lo = _lane_halves(x.shape)
    return jnp.where(lo, x, fill).astype(BF16), jnp.where(lo, fill, x).astype(BF16)


def _with_ones(v):
    return _pair_halves(v, 1.0)


def _ffn_half_step(x, g_ref, wgu_ref, wd_ref, act_ref):
    xn = _rms(x, g_ref[...]).astype(BF16)
    for c in range(D_FF // FF_CHUNK):
        lo = c * FF_CHUNK
        gate = _dot(xn, wgu_ref[:, lo:lo + FF_CHUNK])
        up = _dot(xn, wgu_ref[:, D_FF + lo:D_FF + lo + FF_CHUNK])
        act_ref[:, lo:lo + FF_CHUNK] = (gate * jax.nn.sigmoid(gate) * up).astype(BF16)
    return x + FFN_RES_SCALE * _dot(act_ref[...], wd_ref[...])


def _pre_even_body(h_ref, ga_ref, wgu_ref, wd_ref, gm_ref, win_ref, cqn_ref, wuq_ref, wuqs_ref,
                   ckvn_ref, wk_ref, wv_ref, cq_ref, sq_ref, ck_ref, sk_ref,
                   h1_ref, aq_ref, ak_ref, avlo_ref, avhi_ref, mq_ref, mk_ref, mvlo_ref, mvhi_ref, act_ref):
    y = _ffn_half_step(h_ref[...], ga_ref, wgu_ref, wd_ref, act_ref)
    h1_ref[...] = y
    xn = _rms(y, gm_ref[...]).astype(BF16)
    z = _dot(xn, win_ref[...])
    o_ak = A_HEADS * A_HEAD_DIM
    o_av = o_ak + LANES
    o_cq = o_av + LANES
    o_ckv = o_cq + B_Q_LORA
    o_kr = o_ckv + B_KV_LORA
    o_krs = o_kr + LANES
    aq_ref[...] = (z[:, :o_ak] * LOG2E).astype(BF16)
    ak_ref[...] = z[:, o_ak:o_av].astype(BF16)
    avlo_ref[...], avhi_ref[...] = _with_ones(z[:, o_av:o_cq])
    cqn = _rms(z[:, o_cq:o_ckv], cqn_ref[...]).astype(BF16)
    ckvn = _rms(z[:, o_ckv:o_kr], ckvn_ref[...]).astype(BF16)
    k_rope = z[:, o_kr:o_krs] * ck_ref[...] + z[:, o_krs:o_krs + LANES] * sk_ref[...]
    q_plain = _dot(cqn, wuq_ref[...])
    q_swap = _dot(cqn, wuqs_ref[...])
    k_nope = _dot(ckvn, wk_ref[...])
    cq = cq_ref[...]
    sq = sq_ref[...]
    for h in range(B_HEADS):
        sl = slice(h * LANES, (h + 1) * LANES)
        mq_ref[:, sl] = (q_plain[:, sl] * cq + q_swap[:, sl] * sq).astype(BF16)
        mk_ref[:, sl] = (k_nope[:, sl] + k_rope).astype(BF16)
    v = _dot(ckvn, wv_ref[...])
    for grp in range(B_HEADS // 2):
        sl = slice(grp * LANES, (grp + 1) * LANES)
        mvlo_ref[:, sl], mvhi_ref[:, sl] = _with_ones(v[:, sl])


def _pre_even(h, layer, ga, wgu, wd, gm, small, tabs, seq):
    t, d = h.shape
    tm = TOKEN_TILE
    n_pos = seq // tm
    row = lambda w: pl.BlockSpec((tm, w), lambda i: (i, 0))
    tab = pl.BlockSpec((tm, LANES), lambda i: (i % n_pos, 0))
    widths = (A_HEADS * A_HEAD_DIM, LANES, LANES, LANES, B_HEADS * LANES, B_HEADS * LANES,
              B_HEADS * B_V_DIM, B_HEADS * B_V_DIM)
    stacks = (ga, wgu, wd, gm)
    return pl.pallas_call(
        _pre_even_body,
        grid=(t // tm,),
        in_specs=[row(d)] + [_layer_resident(a, layer) for a in stacks]
        + [_resident(a.shape) for a in small] + [tab] * 4,
        out_specs=[row(d)] + [row(w) for w in widths],
        out_shape=[jax.ShapeDtypeStruct((t, d), F32)] + [jax.ShapeDtypeStruct((t, w), BF16) for w in widths],
        scratch_shapes=[pltpu.VMEM((tm, D_FF), BF16)],
        compiler_params=_params("parallel"),
        name="pre_even",
    )(h, *stacks, *small, *tabs)


def _pre_odd_body(h_ref, ga_ref, wgu_ref, wd_ref, gm_ref, win_ref, bf_ref,
                  h1_ref, qlo_ref, qhi_ref, k_ref, vlo_ref, vhi_ref, ct_ref, act_ref, carry_ref):
    @pl.when(pl.program_id(1) == 0)
    def _():
        carry_ref[...] = jnp.zeros_like(carry_ref)

    y = _ffn_half_step(h_ref[0], ga_ref, wgu_ref, wd_ref, act_ref)
    h1_ref[0] = y
    xn = _rms(y, gm_ref[...]).astype(BF16)
    w = C_HEADS * C_HEAD_DIM
    q = _dot(xn, win_ref[:, 0:w]) * (C_HEAD_DIM ** -0.5 * LOG2E)
    k_ref[0] = _dot(xn, win_ref[:, w:2 * w]).astype(BF16)
    v = _dot(xn, win_ref[:, 2 * w:3 * w])
    for grp in range(C_HEADS // 2):
        sl = slice(grp * LANES, (grp + 1) * LANES)
        qlo_ref[0, :, sl], qhi_ref[0, :, sl] = _pair_halves(q[:, sl], 0.0)
        vlo_ref[0, :, sl], vhi_ref[0, :, sl] = _with_ones(v[:, sl])
    f = _dot(xn, win_ref[:, 3 * w:3 * w + LANES]) + bf_ref[...]
    c = jnp.minimum(f, 0.0) - jnp.log1p(jnp.exp(-jnp.abs(f)))
    tm = c.shape[0]
    rows = lax.broadcasted_iota(jnp.int32, c.shape, 0)
    shift = 1
    while shift < tm:
        c = c + jnp.where(rows >= shift, pltpu.roll(c, shift, axis=0), 0.0)
        shift *= 2
    c = c + carry_ref[...]
    carry_ref[...] = c[tm - 1:tm, :]
    ct_ref[0] = jnp.transpose(c * LOG2E)[:C_HEADS, :]


def _pre_odd(h3, layer, ga, wgu, wd, gm, win, bf):
    b, s, d = h3.shape
    tm = TOKEN_TILE
    w = C_HEADS * C_HEAD_DIM
    blk = lambda n: pl.BlockSpec((1, tm, n), lambda bi, si: (bi, si, 0))
    stacks = (ga, wgu, wd, gm)
    return pl.pallas_call(
        _pre_odd_body,
        grid=(b, s // tm),
        in_specs=[blk(d)] + [_layer_resident(a, layer) for a in stacks] + [_resident(win.shape), _resident(bf.shape)],
        out_specs=[blk(d)] + [blk(w)] * 5 + [pl.BlockSpec((1, C_HEADS, tm), lambda bi, si: (bi, 0, si))],
        out_shape=[jax.ShapeDtypeStruct((b, s, d), F32)] + [jax.ShapeDtypeStruct((b, s, w), BF16)] * 5
        + [jax.ShapeDtypeStruct((b, C_HEADS, s), F32)],
        scratch_shapes=[pltpu.VMEM((tm, D_FF), BF16), pltpu.VMEM((1, LANES), F32)],
        compiler_params=_params("parallel", "arbitrary"),
        name="pre_odd",
    )(h3, *stacks, win, bf)


def _post_body(*refs, n_in, final):
    h_ref = refs[0]
    x_refs = refs[1:1 + n_in]
    wo_refs = refs[1 + n_in:1 + 2 * n_in]
    gb_ref, wgu_ref, wd_ref, p_ref, gp_ref, wg_ref, wp_ref, gf_ref, o_ref, act_ref = refs[1 + 2 * n_in:]
    y = h_ref[...]
    for x_ref, wo_ref in zip(x_refs, wo_refs):
        y = y + _dot(x_ref[...], wo_ref[...])
    y = _ffn_half_step(y, gb_ref, wgu_ref, wd_ref, act_ref)
    gate = jax.nn.sigmoid(_dot(_rms(y, gp_ref[...]).astype(BF16), wg_ref[...]))
    y = y + gate * _dot(p_ref[...].astype(BF16), wp_ref[...])
    if final:
        y = _rms(y, gf_ref[...])
    o_ref[...] = y


def _post(h, xs, wos, layer, gb, wgu, wd, p_all, gp, wg, wp, gf, final):
    t, d = h.shape
    tm = TOKEN_TILE
    row = lambda w: pl.BlockSpec((tm, w), lambda i: (i, 0))
    return pl.pallas_call(
        functools.partial(_post_body, n_in=len(xs), final=final),
        grid=(t // tm,),
        in_specs=[row(d)] + [row(x.shape[1]) for x in xs] + [_resident(w.shape) for w in wos]
        + [_layer_resident(a, layer) for a in (gb, wgu, wd)]
        + [pl.BlockSpec((None, tm, PLE_DIM), lambda i: (layer, i, 0))]
        + [_layer_resident(a, layer) for a in (gp, wg, wp)] + [_resident(gf.shape)],
        out_specs=row(d),
        out_shape=jax.ShapeDtypeStruct((t, d), F32),
        scratch_shapes=[pltpu.VMEM((tm, D_FF), BF16)],
        compiler_params=_params("parallel"),
        name="post",
    )(h, *xs, *wos, gb, wgu, wd, p_all, gp, wg, wp, gf)


def _split_pair(q):
    lo = _lane_halves(q.shape)
    qf = q.astype(F32)
    return jnp.where(lo, qf, 0.0).astype(BF16), jnp.where(lo, 0.0, qf).astype(BF16)


def _lane_chunks(s):
    return [s[:, c * LANES:(c + 1) * LANES] for c in range(s.shape[1] // LANES)]


def _online_step(chunks, v_ones, state):
    m_prev, acc = state
    m_new = jnp.maximum(m_prev, jnp.max(functools.reduce(jnp.maximum, chunks), axis=-1, keepdims=True))
    alpha = jnp.exp2(m_prev - m_new)
    p = jnp.concatenate([jnp.exp2(c - m_new).astype(BF16) for c in chunks], axis=1)
    return m_new, alpha * acc + _dot(p, v_ones)


def _merge_pair(acc_lo, acc_hi):
    n0 = acc_lo / pltpu.roll(acc_lo, HALF, axis=1)
    n1 = acc_hi / pltpu.roll(acc_hi, HALF, axis=1)
    return jnp.where(_lane_halves(acc_lo.shape), n0, n1)


def _attend(blk, score_fn, v_fn, s_refs, m_ref, acc_ref):
    t = ATT_TILE
    heads = range(m_ref.shape[0])
    m_ref[...] = jnp.full(m_ref.shape, -jnp.inf, F32)
    acc_ref[...] = jnp.zeros(acc_ref.shape, F32)

    def scores(j, s_ref):
        off = pl.multiple_of(j * t, t)
        for hh in heads:
            s_ref[hh] = score_fn(hh, 0, t, off, t)

    def consume(j, s_ref):
        off = pl.multiple_of(j * t, t)
        for hh in heads:
            m_ref[hh], acc_ref[hh] = _online_step(_lane_chunks(s_ref[hh]), v_fn(hh, off, t),
                                                  (m_ref[hh], acc_ref[hh]))

    strip = t // ATT_DIAG_STRIPS
    row_strips = tuple((r * strip, (r + 1) * strip) for r in range(ATT_DIAG_STRIPS))

    def scores_diagonal(s_ref):
        off = pl.multiple_of(blk * t, t)
        for hh in heads:
            for row0, n_keys in row_strips:
                s_ref[hh, row0:row0 + strip, 0:n_keys] = score_fn(hh, row0, strip, off, n_keys)

    def consume_diagonal(s_ref):
        off = pl.multiple_of(blk * t, t)
        tri = (lax.broadcasted_iota(jnp.int32, (strip, strip), 1)
               <= lax.broadcasted_iota(jnp.int32, (strip, strip), 0))
        for hh in heads:
            for row0, n_keys in row_strips:
                rows = slice(row0, row0 + strip)
                square = jnp.where(tri, s_ref[hh, rows, n_keys - strip:n_keys], -jnp.inf)
                chunks = _lane_chunks(s_ref[hh, rows, 0:n_keys - strip]) + _lane_chunks(square)
                m_ref[hh, rows], acc_ref[hh, rows] = _online_step(chunks, v_fn(hh, off, n_keys),
                                                                  (m_ref[hh, rows], acc_ref[hh, rows]))

    s_even, s_odd = s_refs
    scores(0, s_even)

    def two_tiles(pair, carry):
        j = 2 * pair
        scores(j + 1, s_odd)
        consume(j, s_even)
        scores(j + 2, s_even)
        consume(j + 1, s_odd)
        return carry

    lax.fori_loop(0, blk // 2, two_tiles, 0)

    @pl.when(blk % 2 == 1)
    def _():
        scores_diagonal(s_odd)
        consume(blk - 1, s_even)
        consume_diagonal(s_odd)

    @pl.when(blk % 2 == 0)
    def _():
        consume_diagonal(s_even)

    return [_merge_pair(acc_ref[2 * g], acc_ref[2 * g + 1]) for g in range(len(heads) // 2)]


def _attend_scratch(n_pairs):
    t, n = ATT_TILE, 2 * n_pairs
    return [pltpu.VMEM((n, t, t), F32), pltpu.VMEM((n, t, t), F32),
            pltpu.VMEM((n, t, LANES), F32), pltpu.VMEM((n, t, LANES), F32)]


def _pair_values(vlo_ref, vhi_ref):
    def v_fn(hh, key0, n_keys):
        g, ref = hh // 2, (vlo_ref, vhi_ref)[hh % 2]
        return ref[0, pl.ds(key0, n_keys), g * LANES:(g + 1) * LANES]
    return v_fn


def _mla_body(q_ref, k_ref, vlo_ref, vhi_ref, o_ref, s0_ref, s1_ref, m_ref, acc_ref):
    def score_fn(hh, row0, n_rows, key0, n_keys):
        sl = slice(hh * LANES, (hh + 1) * LANES)
        return _dot_nt(q_ref[0, row0:row0 + n_rows, sl], k_ref[0, pl.ds(key0, n_keys), sl])

    outs = _attend(pl.program_id(2), score_fn, _pair_values(vlo_ref, vhi_ref), (s0_ref, s1_ref), m_ref, acc_ref)
    for g, out in enumerate(outs):
        o_ref[0, :, g * LANES:(g + 1) * LANES] = out.astype(BF16)


def _attention_specs(s, width):
    rows = pl.BlockSpec((1, ATT_TILE, width), lambda bi, hg, qi: (bi, qi, hg))
    seq = pl.BlockSpec((1, s, width), lambda bi, hg, qi: (bi, 0, hg))
    return rows, seq


def _mla_attention(q, k, v_lo, v_hi):
    b, s, _ = q.shape
    head_rows, head_seq = _attention_specs(s, MLA_PAIRS * 2 * LANES)
    pair_rows, pair_seq = _attention_specs(s, MLA_PAIRS * LANES)
    return pl.pallas_call(
        _mla_body,
        grid=(b, B_HEADS // (2 * MLA_PAIRS), s // ATT_TILE),
        in_specs=[head_rows, head_seq, pair_seq, pair_seq],
        out_specs=pair_rows,
        out_shape=jax.ShapeDtypeStruct((b, s, B_HEADS * B_V_DIM), BF16),
        scratch_shapes=_attend_scratch(MLA_PAIRS),
        compiler_params=_params("parallel", "parallel", "arbitrary"),
        name="mla_attention",
    )(q, k, v_lo, v_hi)


def _fox_body(qlo_ref, qhi_ref, k_ref, vlo_ref, vhi_ref, c_ref, o_ref, s0_ref, s1_ref, m_ref, acc_ref):
    t = ATT_TILE
    blk = pl.program_id(2)
    rows = pl.ds(pl.multiple_of(blk * t, t), t)
    cqs = [jnp.transpose(jnp.broadcast_to(c_ref[0, hh // 2, hh % 2:hh % 2 + 1, rows], (LANES, t)))
           for hh in range(2 * FOX_PAIRS)]

    def score_fn(hh, row0, n_rows, key0, n_keys):
        g, q_ref = hh // 2, (qlo_ref, qhi_ref)[hh % 2]
        lanes = slice(g * LANES, (g + 1) * LANES)
        s = _dot_nt(q_ref[0, row0:row0 + n_rows, lanes], k_ref[0, pl.ds(key0, n_keys), lanes])
        cq = cqs[hh][row0:row0 + n_rows]
        ck = c_ref[0, g, hh % 2:hh % 2 + 1, pl.ds(key0, n_keys)]
        return jnp.concatenate([c + cq - ck[:, n * LANES:(n + 1) * LANES]
                                for n, c in enumerate(_lane_chunks(s))], axis=1)

    outs = _attend(blk, score_fn, _pair_values(vlo_ref, vhi_ref), (s0_ref, s1_ref), m_ref, acc_ref)
    for g, out in enumerate(outs):
        o_ref[0, :, g * LANES:(g + 1) * LANES] = out.astype(BF16)


def _fox_attention(q_lo, q_hi, k, v_lo, v_hi, c_t):
    b, s, w = k.shape
    pair_rows, pair_seq = _attention_specs(s, FOX_PAIRS * LANES)
    gates = pl.BlockSpec((1, FOX_PAIRS, 2, s), lambda bi, hg, qi: (bi, hg, 0, 0))
    return pl.pallas_call(
        _fox_body,
        grid=(b, C_HEADS // (2 * FOX_PAIRS), s // ATT_TILE),
        in_specs=[pair_rows, pair_rows, pair_seq, pair_seq, pair_seq, gates],
        out_specs=pair_rows,
        out_shape=jax.ShapeDtypeStruct((b, s, w), BF16),
        scratch_shapes=_attend_scratch(FOX_PAIRS),
        compiler_params=_params("parallel", "parallel", "arbitrary"),
        name="fox_attention",
    )(q_lo, q_hi, k, v_lo, v_hi, c_t)


def _swa_body(sink_ref, q_ref, k_ref, vlo_ref, vhi_ref, bias_ref, o_ref):
    w = WINDOW
    n_sub = SWA_TQ // w
    n_groups = A_HEADS // 2
    v_refs = (vlo_ref, vhi_ref)
    for sb in range(n_sub):
        blk = pl.program_id(1) * n_sub + sb
        k_start = pl.multiple_of(jnp.maximum(blk - 1, 0) * w, w)
        table = jnp.minimum(blk, 1)
        k_tile = k_ref[0, pl.ds(k_start, 2 * w), :]
        heads = [(grp, half) for grp in range(n_groups) for half in range(2)]
        scores = []
        for grp, half in heads:
            q_half = _split_pair(q_ref[0, sb * w:(sb + 1) * w, grp * LANES:(grp + 1) * LANES])[half]
            scores.append(_dot_nt(q_half, k_tile) + bias_ref[table, grp + n_groups * half])
        probs = []
        for (grp, half), s in zip(heads, scores):
            chunks = _lane_chunks(s)
            sink = sink_ref[grp + n_groups * half]
            row_max = jnp.max(functools.reduce(jnp.maximum, chunks), axis=-1, keepdims=True)
            m = jnp.maximum(jnp.broadcast_to(row_max, (w, LANES)), sink)
            e = jnp.concatenate([jnp.exp2(c - m).astype(BF16) for c in chunks], axis=1)
            probs.append((e, jnp.exp2(sink - m)))
        out = []
        for (grp, half), (e, sink_term) in zip(heads, probs):
            pv = _dot(e, v_refs[half][0, pl.ds(k_start, 2 * w), :])
            out.append(pv / (pltpu.roll(pv, HALF, axis=1) + sink_term))
        for grp in range(n_groups):
            o_ref[0, sb * w:(sb + 1) * w, grp * LANES:(grp + 1) * LANES] = jnp.where(
                _lane_halves(out[0].shape), out[2 * grp], out[2 * grp + 1]).astype(BF16)


def _swa_bias_tables():
    w = WINDOW
    row = lax.broadcasted_iota(jnp.int32, (w, 2 * w), 0)
    col = lax.broadcasted_iota(jnp.int32, (w, 2 * w), 1)
    slopes = 2.0 ** (-8.0 * jnp.arange(1, A_HEADS + 1, dtype=F32) / A_HEADS)
    tables = []
    for dist in (row - col, row + w - col):
        band = (dist >= 0) & (dist < w)
        alibi = -LOG2E * slopes[:, None, None] * dist.astype(F32)[None]
        tables.append(jnp.where(band[None], alibi, -jnp.inf))
    return jnp.stack(tables)


def _swa_attention(q, k, v_lo, v_hi, sinks):
    b, s, wq = q.shape
    tq = SWA_TQ
    bias = _swa_bias_tables()
    rows = pl.BlockSpec((1, tq, wq), lambda bi, qi: (bi, qi, 0))
    seq = pl.BlockSpec((1, s, LANES), lambda bi, qi: (bi, 0, 0))
    return pl.pallas_call(
        _swa_body,
        grid=(b, s // tq),
        in_specs=[pl.BlockSpec(memory_space=pltpu.SMEM), rows, seq, seq, seq, _resident(bias.shape)],
        out_specs=rows,
        out_shape=jax.ShapeDtypeStruct((b, s, wq), BF16),
        compiler_params=_params("parallel", "arbitrary"),
        name="swa_attention",
    )(sinks, q, k, v_lo, v_hi, bias)


def _swa_head_perm():
    n_groups = A_HEADS // 2
    idx = []
    for grp in range(n_groups):
        for head in (grp, grp + n_groups):
            idx.extend(range(head * A_HEAD_DIM, (head + 1) * A_HEAD_DIM))
    return np.asarray(idx, np.int32)


def _pad_cols(w, width):
    return jnp.pad(w, ((0, 0), (0, width - w.shape[1])))


def _even_weights(w_in, w_uq, w_ukv, w_out):
    perm = _swa_head_perm()
    nq = A_HEADS * A_HEAD_DIM
    nkv = A_KV_HEADS * A_HEAD_DIM
    o_cq = nq + 2 * nkv
    o_ckv = o_cq + B_Q_LORA
    o_kr = o_ckv + B_KV_LORA
    half = B_ROPE_DIM // 2
    aq = w_in[:, :nq][:, perm] * (A_HEAD_DIM ** -0.5)
    kr = w_in[:, o_kr:o_kr + B_ROPE_DIM]
    kr_swap = jnp.concatenate([kr[:, half:], kr[:, :half]], axis=1)
    place = lambda x: jnp.pad(x, ((0, 0), (B_NOPE_DIM, LANES - B_NOPE_DIM - B_ROPE_DIM)))
    win = jnp.concatenate([aq, w_in[:, nq:o_kr], place(kr), place(kr_swap)], axis=1).astype(BF16)

    dq = B_NOPE_DIM + B_ROPE_DIM
    uq = w_uq.reshape(B_Q_LORA, B_HEADS, dq)
    uq_swap = jnp.concatenate([jnp.zeros_like(uq[..., :B_NOPE_DIM]),
                               uq[..., B_NOPE_DIM + half:], uq[..., B_NOPE_DIM:B_NOPE_DIM + half]], axis=-1)
    pad_head = lambda x: jnp.pad(x, ((0, 0), (0, 0), (0, LANES - x.shape[-1]))).reshape(x.shape[0], B_HEADS * LANES)
    wuq = pad_head(uq).astype(BF16)
    wuqs = pad_head(uq_swap).astype(BF16)

    ukv = w_ukv.reshape(B_KV_LORA, B_HEADS, B_NOPE_DIM + B_V_DIM)
    wk = pad_head(ukv[..., :B_NOPE_DIM]).astype(BF16)
    wv = ukv[..., B_NOPE_DIM:].reshape(B_KV_LORA, B_HEADS * B_V_DIM).astype(BF16)

    wo_a = w_out[:nq][perm].astype(BF16)
    wo_b = w_out[nq:].astype(BF16)
    return win, wuq, wuqs, wk, wv, wo_a, wo_b


def _rope_tables(seq):
    inv = ROPE_THETA ** (-jnp.arange(0, B_ROPE_DIM, 2, dtype=F32) / B_ROPE_DIM)
    ang = jnp.arange(seq, dtype=F32)[:, None] * inv[None, :]
    cos, sin = jnp.cos(ang), jnp.sin(ang)
    ones = jnp.ones((seq, B_NOPE_DIM), F32)
    zeros = jnp.zeros((seq, B_NOPE_DIM), F32)
    tail = jnp.zeros((seq, LANES - B_NOPE_DIM - B_ROPE_DIM), F32)
    c_tab = jnp.concatenate([ones, cos, cos, tail], axis=1)
    s_tab = jnp.concatenate([zeros, -sin, sin, tail], axis=1)
    scale = (B_NOPE_DIM + B_ROPE_DIM) ** -0.5 * LOG2E
    return c_tab * scale, s_tab * scale, c_tab, s_tab


def _odd_weights(w_in, b_f):
    w = C_HEADS * C_HEAD_DIM
    win = jnp.concatenate([w_in[:, :3 * w], _pad_cols(w_in[:, 3 * w:], LANES)], axis=1).astype(BF16)
    bf = _pad_cols(b_f.reshape(1, C_HEADS).astype(F32), LANES)
    return win, bf


def kernel(x, p, ffa_norm, ffa_w_gate_up, ffa_w_down, mix_norm, ffb_norm, ffb_w_gate_up, ffb_w_down, ple_norm, ple_w_gate, ple_w_proj, ev_w_in, ev_sinks, ev_cq_norm, ev_w_uq, ev_ckv_norm, ev_w_ukv, ev_w_out, od_w_in, od_b_f, od_w_out, final_norm):
    batch, seq, d = x.shape
    depth = p.shape[0]
    t = batch * seq
    h = x.reshape(t, d)
    p_all = p.reshape(depth, t, PLE_DIM)
    vec = lambda a: a.reshape(1, -1).astype(F32)
    vecs = lambda a: a.reshape(a.shape[0], 1, a.shape[1]).astype(F32)
    tabs = _rope_tables(seq)
    ga, gm, gb, gp = vecs(ffa_norm), vecs(mix_norm), vecs(ffb_norm), vecs(ple_norm)
    wa_gu, wa_d = ffa_w_gate_up.astype(BF16), ffa_w_down.astype(BF16)
    wb_gu, wb_d = ffb_w_gate_up.astype(BF16), ffb_w_down.astype(BF16)
    wg, wp = ple_w_gate.astype(BF16), ple_w_proj.astype(BF16)
    gf = vec(final_norm)

    for i in range(depth):
        j = i // 2
        if i % 2 == 0:
            win, wuq, wuqs, wk, wv, wo_a, wo_b = _even_weights(ev_w_in[j], ev_w_uq[j], ev_w_ukv[j], ev_w_out[j])
            small = (win, vec(ev_cq_norm[j]), wuq, wuqs, vec(ev_ckv_norm[j]), wk, wv)
            h, aq, ak, av_lo, av_hi, mq, mk, mv_lo, mv_hi = _pre_even(h, i, ga, wa_gu, wa_d, gm, small, tabs, seq)
            b3 = lambda a: a.reshape(batch, seq, a.shape[-1])
            out_a = _swa_attention(b3(aq), b3(ak), b3(av_lo), b3(av_hi), ev_sinks[j].astype(F32) * LOG2E)
            out_b = _mla_attention(b3(mq), b3(mk), b3(mv_lo), b3(mv_hi))
            xs, wos = [out_a.reshape(t, -1), out_b.reshape(t, -1)], [wo_a, wo_b]
        else:
            win, bf = _odd_weights(od_w_in[j], od_b_f[j])
            h3, q_lo, q_hi, k, v_lo, v_hi, c_t = _pre_odd(h.reshape(batch, seq, d), i, ga, wa_gu, wa_d, gm, win, bf)
            h = h3.reshape(t, d)
            out = _fox_attention(q_lo, q_hi, k, v_lo, v_hi, c_t.reshape(batch, C_HEADS // 2, 2, seq))
            xs, wos = [out.reshape(t, -1)], [od_w_out[j].astype(BF16)]
        h = _post(h, xs, wos, i, gb, wb_gu, wb_d, p_all, gp, wg, wp, gf, final=(i == depth - 1))
    return h.reshape(batch, seq, d)
```

```python
import functools
import math

import numpy as np
import jax
import jax.numpy as jnp
from jax import lax
from jax.experimental import pallas as pl
from jax.experimental.pallas import tpu as pltpu

F32 = jnp.float32
BF16 = jnp.bfloat16

D_MODEL = 1024
D_FF = 2816
FFN_RES_SCALE = 0.5
RMS_EPS = 1e-6
PLE_DIM = 256
A_HEADS = 8
A_KV_HEADS = 2
A_HEAD_DIM = 64
WINDOW = 128
B_HEADS = 8
B_Q_LORA = 256
B_KV_LORA = 128
B_NOPE_DIM = 64
B_ROPE_DIM = 32
B_V_DIM = 64
ROPE_THETA = 10000.0
C_HEADS = 16
C_HEAD_DIM = 64

LOG2E = math.log2(math.e)
LANES = 128
HALF = LANES // 2
VMEM_LIMIT = 64 * 1024 * 1024

TOKEN_TILE = 512
FF_CHUNK = 256
ATT_TILE = 1024
MLA_PAIRS = 1
FOX_PAIRS = 2
ATT_DIAG_STRIPS = 4
SWA_TQ = 512


def _resident(shape):
    nd = len(shape)
    return pl.BlockSpec(shape, lambda *_: (0,) * nd, pipeline_mode=pl.Buffered(1))


def _layer_resident(stack, layer):
    _, a, b = stack.shape
    return pl.BlockSpec((None, a, b), lambda *_: (layer, 0, 0), pipeline_mode=pl.Buffered(1))


def _params(*sem):
    return pltpu.CompilerParams(dimension_semantics=sem, vmem_limit_bytes=VMEM_LIMIT)


def _rms(x, g):
    ms = jnp.mean(x * x, axis=-1, keepdims=True)
    return x * lax.rsqrt(ms + RMS_EPS) * g


def _dot(a, b):
    return jnp.dot(a, b, preferred_element_type=F32)


def _dot_nt(a, b):
    return lax.dot_general(a, b, (((1,), (1,)), ((), ())), preferred_element_type=F32)


def _lane_halves(shape):
    lane = lax.broadcasted_iota(jnp.int32, shape, len(shape) - 1)
    return lane < HALF


def _pair_halves(x, fill):
    lo = _lane_halves(x.shape)
    return jnp.where(lo, x, fill).astype(BF16), jnp.where(lo, fill, x).astype(BF16)


def _with_ones(v):
    return _pair_halves(v, 1.0)


def _ffn_half_step(x, g_ref, wgu_ref, wd_ref, act_ref):
    xn = _rms(x, g_ref[...]).astype(BF16)
    for c in range(D_FF // FF_CHUNK):
        lo = c * FF_CHUNK
        gate = _dot(xn, wgu_ref[:, lo:lo + FF_CHUNK])
        up = _dot(xn, wgu_ref[:, D_FF + lo:D_FF + lo + FF_CHUNK])
        act_ref[:, lo:lo + FF_CHUNK] = (gate * jax.nn.sigmoid(gate) * up).astype(BF16)
    return x + FFN_RES_SCALE * _dot(act_ref[...], wd_ref[...])


def _swap_rope_halves(x):
    lane = lax.broadcasted_iota(jnp.int32, x.shape, 1)
    half = B_ROPE_DIM // 2
    take_upper = pltpu.roll(x, LANES - half, axis=1)
    take_lower = pltpu.roll(x, half, axis=1)
    return jnp.where(lane < B_NOPE_DIM + half, take_upper, take_lower)


def _pre_even_body(h_ref, ga_ref, wgu_ref, wd_ref, gm_ref, win_ref, cqn_ref, wuq_ref, wuqs_ref,
                   ckvn_ref, wk_ref, wv_ref, cq_ref, sq_ref, ck_ref, sk_ref,
                   h1_ref, aq_ref, ak_ref, avlo_ref, avhi_ref, mq_ref, mk_ref, mvlo_ref, mvhi_ref, act_ref):
    y = _ffn_half_step(h_ref[...], ga_ref, wgu_ref, wd_ref, act_ref)
    h1_ref[...] = y
    xn = _rms(y, gm_ref[...]).astype(BF16)
    z = _dot(xn, win_ref[...])
    o_ak = A_HEADS * A_HEAD_DIM
    o_av = o_ak + LANES
    o_cq = o_av + LANES
    o_ckv = o_cq + B_Q_LORA
    o_kr = o_ckv + B_KV_LORA
    aq_ref[...] = (z[:, :o_ak] * LOG2E).astype(BF16)
    ak_ref[...] = z[:, o_ak:o_av].astype(BF16)
    avlo_ref[...], avhi_ref[...] = _with_ones(z[:, o_av:o_cq])
    cqn = _rms(z[:, o_cq:o_ckv], cqn_ref[...]).astype(BF16)
    ckvn = _rms(z[:, o_ckv:o_kr], ckvn_ref[...]).astype(BF16)
    k_plain = z[:, o_kr:o_kr + LANES]
    k_rope = k_plain * ck_ref[...] + _swap_rope_halves(k_plain) * sk_ref[...]
    q_plain = _dot(cqn, wuq_ref[...])
    q_swap = _dot(cqn, wuqs_ref[...])
    k_nope = _dot(ckvn, wk_ref[...])
    cq = cq_ref[...]
    sq = sq_ref[...]
    for h in range(B_HEADS):
        sl = slice(h * LANES, (h + 1) * LANES)
        mq_ref[:, sl] = (q_plain[:, sl] * cq + q_swap[:, sl] * sq).astype(BF16)
        mk_ref[:, sl] = (k_nope[:, sl] + k_rope).astype(BF16)
    v = _dot(ckvn, wv_ref[...])
    for grp in range(B_HEADS // 2):
        sl = slice(grp * LANES, (grp + 1) * LANES)
        mvlo_ref[:, sl], mvhi_ref[:, sl] = _with_ones(v[:, sl])


def _pre_even(h, layer, ga, wgu, wd, gm, small, tabs, seq):
    t, d = h.shape
    tm = TOKEN_TILE
    n_pos = seq // tm
    row = lambda w: pl.BlockSpec((tm, w), lambda i: (i, 0))
    tab = pl.BlockSpec((tm, LANES), lambda i: (i % n_pos, 0))
    widths = (A_HEADS * A_HEAD_DIM, LANES, LANES, LANES, B_HEADS * LANES, B_HEADS * LANES,
              B_HEADS * B_V_DIM, B_HEADS * B_V_DIM)
    stacks = (ga, wgu, wd, gm)
    return pl.pallas_call(
        _pre_even_body,
        grid=(t // tm,),
        in_specs=[row(d)] + [_layer_resident(a, layer) for a in stacks]
        + [_resident(a.shape) for a in small] + [tab] * 4,
        out_specs=[row(d)] + [row(w) for w in widths],
        out_shape=[jax.ShapeDtypeStruct((t, d), F32)] + [jax.ShapeDtypeStruct((t, w), BF16) for w in widths],
        scratch_shapes=[pltpu.VMEM((tm, D_FF), BF16)],
        compiler_params=_params("parallel"),
        name="pre_even",
    )(h, *stacks, *small, *tabs)


def _pre_odd_body(h_ref, ga_ref, wgu_ref, wd_ref, gm_ref, win_ref, bf_ref,
                  h1_ref, qlo_ref, qhi_ref, k_ref, vlo_ref, vhi_ref, ct_ref, act_ref, carry_ref):
    @pl.when(pl.program_id(1) == 0)
    def _():
        carry_ref[...] = jnp.zeros_like(carry_ref)

    y = _ffn_half_step(h_ref[0], ga_ref, wgu_ref, wd_ref, act_ref)
    h1_ref[0] = y
    xn = _rms(y, gm_ref[...]).astype(BF16)
    w = C_HEADS * C_HEAD_DIM
    q = _dot(xn, win_ref[:, 0:w]) * (C_HEAD_DIM ** -0.5 * LOG2E)
    k_ref[0] = _dot(xn, win_ref[:, w:2 * w]).astype(BF16)
    v = _dot(xn, win_ref[:, 2 * w:3 * w])
    for grp in range(C_HEADS // 2):
        sl = slice(grp * LANES, (grp + 1) * LANES)
        qlo_ref[0, :, sl], qhi_ref[0, :, sl] = _pair_halves(q[:, sl], 0.0)
        vlo_ref[0, :, sl], vhi_ref[0, :, sl] = _with_ones(v[:, sl])
    f = _dot(xn, win_ref[:, 3 * w:3 * w + LANES]) + bf_ref[...]
    c = jnp.minimum(f, 0.0) - jnp.log1p(jnp.exp(-jnp.abs(f)))
    tm = c.shape[0]
    rows = lax.broadcasted_iota(jnp.int32, c.shape, 0)
    shift = 1
    while shift < tm:
        c = c + jnp.where(rows >= shift, pltpu.roll(c, shift, axis=0), 0.0)
        shift *= 2
    c = c + carry_ref[...]
    carry_ref[...] = c[tm - 1:tm, :]
    ct_ref[0] = jnp.transpose(c * LOG2E)[:C_HEADS, :]


def _pre_odd(h3, layer, ga, wgu, wd, gm, win, bf):
    b, s, d = h3.shape
    tm = TOKEN_TILE
    w = C_HEADS * C_HEAD_DIM
    blk = lambda n: pl.BlockSpec((1, tm, n), lambda bi, si: (bi, si, 0))
    stacks = (ga, wgu, wd, gm)
    return pl.pallas_call(
        _pre_odd_body,
        grid=(b, s // tm),
        in_specs=[blk(d)] + [_layer_resident(a, layer) for a in stacks] + [_resident(win.shape), _resident(bf.shape)],
        out_specs=[blk(d)] + [blk(w)] * 5 + [pl.BlockSpec((1, C_HEADS, tm), lambda bi, si: (bi, 0, si))],
        out_shape=[jax.ShapeDtypeStruct((b, s, d), F32)] + [jax.ShapeDtypeStruct((b, s, w), BF16)] * 5
        + [jax.ShapeDtypeStruct((b, C_HEADS, s), F32)],
        scratch_shapes=[pltpu.VMEM((tm, D_FF), BF16), pltpu.VMEM((1, LANES), F32)],
        compiler_params=_params("parallel", "arbitrary"),
        name="pre_odd",
    )(h3, *stacks, win, bf)


def _post_body(*refs, n_in, final):
    h_ref = refs[0]
    x_refs = refs[1:1 + n_in]
    wo_refs = refs[1 + n_in:1 + 2 * n_in]
    gb_ref, wgu_ref, wd_ref, p_ref, gp_ref, wg_ref, wp_ref, gf_ref, o_ref, act_ref = refs[1 + 2 * n_in:]
    y = h_ref[...]
    for x_ref, wo_ref in zip(x_refs, wo_refs):
        y = y + _dot(x_ref[...], wo_ref[...])
    y = _ffn_half_step(y, gb_ref, wgu_ref, wd_ref, act_ref)
    gate = jax.nn.sigmoid(_dot(_rms(y, gp_ref[...]).astype(BF16), wg_ref[...]))
    y = y + gate * _dot(p_ref[...].astype(BF16), wp_ref[...])
    if final:
        y = _rms(y, gf_ref[...])
    o_ref[...] = y


def _post(h, xs, wos, layer, gb, wgu, wd, p_all, gp, wg, wp, gf, final):
    t, d = h.shape
    tm = TOKEN_TILE
    row = lambda w: pl.BlockSpec((tm, w), lambda i: (i, 0))
    return pl.pallas_call(
        functools.partial(_post_body, n_in=len(xs), final=final),
        grid=(t // tm,),
        in_specs=[row(d)] + [row(x.shape[1]) for x in xs] + [_resident(w.shape) for w in wos]
        + [_layer_resident(a, layer) for a in (gb, wgu, wd)]
        + [pl.BlockSpec((None, tm, PLE_DIM), lambda i: (layer, i, 0))]
        + [_layer_resident(a, layer) for a in (gp, wg, wp)] + [_resident(gf.shape)],
        out_specs=row(d),
        out_shape=jax.ShapeDtypeStruct((t, d), F32),
        scratch_shapes=[pltpu.VMEM((tm, D_FF), BF16)],
        compiler_params=_params("parallel"),
        name="post",
    )(h, *xs, *wos, gb, wgu, wd, p_all, gp, wg, wp, gf)


def _split_pair(q):
    lo = _lane_halves(q.shape)
    qf = q.astype(F32)
    return jnp.where(lo, qf, 0.0).astype(BF16), jnp.where(lo, 0.0, qf).astype(BF16)


def _lane_chunks(s):
    return [s[:, c * LANES:(c + 1) * LANES] for c in range(s.shape[1] // LANES)]


def _online_step(chunks, v_ones, state):
    m_prev, acc = state
    m_new = jnp.maximum(m_prev, jnp.max(functools.reduce(jnp.maximum, chunks), axis=-1, keepdims=True))
    alpha = jnp.exp2(m_prev - m_new)
    p = jnp.concatenate([jnp.exp2(c - m_new).astype(BF16) for c in chunks], axis=1)
    return m_new, alpha * acc + _dot(p, v_ones)


def _merge_pair(acc_lo, acc_hi):
    n0 = acc_lo / pltpu.roll(acc_lo, HALF, axis=1)
    n1 = acc_hi / pltpu.roll(acc_hi, HALF, axis=1)
    return jnp.where(_lane_halves(acc_lo.shape), n0, n1)


def _attend(blk, score_fn, v_fn, s_refs, m_ref, acc_ref):
    t = ATT_TILE
    heads = range(m_ref.shape[0])
    m_ref[...] = jnp.full(m_ref.shape, -jnp.inf, F32)
    acc_ref[...] = jnp.zeros(acc_ref.shape, F32)

    def scores(j, s_ref):
        off = pl.multiple_of(j * t, t)
        for hh in heads:
            s_ref[hh] = score_fn(hh, 0, t, off, t)

    def consume(j, s_ref):
        off = pl.multiple_of(j * t, t)
        for hh in heads:
            m_ref[hh], acc_ref[hh] = _online_step(_lane_chunks(s_ref[hh]), v_fn(hh, off, t),
                                                  (m_ref[hh], acc_ref[hh]))

    strip = t // ATT_DIAG_STRIPS
    row_strips = tuple((r * strip, (r + 1) * strip) for r in range(ATT_DIAG_STRIPS))

    def scores_diagonal(s_ref):
        off = pl.multiple_of(blk * t, t)
        for hh in heads:
            for row0, n_keys in row_strips:
                s_ref[hh, row0:row0 + strip, 0:n_keys] = score_fn(hh, row0, strip, off, n_keys)

    def consume_diagonal(s_ref):
        off = pl.multiple_of(blk * t, t)
        tri = (lax.broadcasted_iota(jnp.int32, (strip, strip), 1)
               <= lax.broadcasted_iota(jnp.int32, (strip, strip), 0))
        for hh in heads:
            for row0, n_keys in row_strips:
                rows = slice(row0, row0 + strip)
                square = jnp.where(tri, s_ref[hh, rows, n_keys - strip:n_keys], -jnp.inf)
                chunks = _lane_chunks(s_ref[hh, rows, 0:n_keys - strip]) + _lane_chunks(square)
                m_ref[hh, rows], acc_ref[hh, rows] = _online_step(chunks, v_fn(hh, off, n_keys),
                                                                  (m_ref[hh, rows], acc_ref[hh, rows]))

    s_even, s_odd = s_refs
    scores(0, s_even)

    def two_tiles(pair, carry):
        j = 2 * pair
        scores(j + 1, s_odd)
        consume(j, s_even)
        scores(j + 2, s_even)
        consume(j + 1, s_odd)
        return carry

    lax.fori_loop(0, blk // 2, two_tiles, 0)

    @pl.when(blk % 2 == 1)
    def _():
        scores_diagonal(s_odd)
        consume(blk - 1, s_even)
        consume_diagonal(s_odd)

    @pl.when(blk % 2 == 0)
    def _():
        consume_diagonal(s_even)

    return [_merge_pair(acc_ref[2 * g], acc_ref[2 * g + 1]) for g in range(len(heads) // 2)]


def _attend_scratch(n_pairs):
    t, n = ATT_TILE, 2 * n_pairs
    return [pltpu.VMEM((n, t, t), F32), pltpu.VMEM((n, t, t), F32),
            pltpu.VMEM((n, t, LANES), F32), pltpu.VMEM((n, t, LANES), F32)]


def _pair_values(vlo_ref, vhi_ref):
    def v_fn(hh, key0, n_keys):
        g, ref = hh // 2, (vlo_ref, vhi_ref)[hh % 2]
        return ref[0, pl.ds(key0, n_keys), g * LANES:(g + 1) * LANES]
    return v_fn


def _mla_body(q_ref, k_ref, vlo_ref, vhi_ref, o_ref, s0_ref, s1_ref, m_ref, acc_ref):
    def score_fn(hh, row0, n_rows, key0, n_keys):
        sl = slice(hh * LANES, (hh + 1) * LANES)
        return _dot_nt(q_ref[0, row0:row0 + n_rows, sl], k_ref[0, pl.ds(key0, n_keys), sl])

    outs = _attend(pl.program_id(2), score_fn, _pair_values(vlo_ref, vhi_ref), (s0_ref, s1_ref), m_ref, acc_ref)
    for g, out in enumerate(outs):
        o_ref[0, :, g * LANES:(g + 1) * LANES] = out.astype(BF16)


def _attention_specs(s, width):
    rows = pl.BlockSpec((1, ATT_TILE, width), lambda bi, hg, qi: (bi, qi, hg))
    seq = pl.BlockSpec((1, s, width), lambda bi, hg, qi: (bi, 0, hg))
    return rows, seq


def _mla_attention(q, k, v_lo, v_hi):
    b, s, _ = q.shape
    head_rows, head_seq = _attention_specs(s, MLA_PAIRS * 2 * LANES)
    pair_rows, pair_seq = _attention_specs(s, MLA_PAIRS * LANES)
    return pl.pallas_call(
        _mla_body,
        grid=(b, B_HEADS // (2 * MLA_PAIRS), s // ATT_TILE),
        in_specs=[head_rows, head_seq, pair_seq, pair_seq],
        out_specs=pair_rows,
        out_shape=jax.ShapeDtypeStruct((b, s, B_HEADS * B_V_DIM), BF16),
        scratch_shapes=_attend_scratch(MLA_PAIRS),
        compiler_params=_params("parallel", "parallel", "arbitrary"),
        name="mla_attention",
    )(q, k, v_lo, v_hi)


def _fox_body(qlo_ref, qhi_ref, k_ref, vlo_ref, vhi_ref, c_ref, o_ref, s0_ref, s1_ref, m_ref, acc_ref):
    t = ATT_TILE
    blk = pl.program_id(2)
    rows = pl.ds(pl.multiple_of(blk * t, t), t)
    cqs = [jnp.transpose(jnp.broadcast_to(c_ref[0, hh // 2, hh % 2:hh % 2 + 1, rows], (LANES, t)))
           for hh in range(2 * FOX_PAIRS)]

    def score_fn(hh, row0, n_rows, key0, n_keys):
        g, q_ref = hh // 2, (qlo_ref, qhi_ref)[hh % 2]
        lanes = slice(g * LANES, (g + 1) * LANES)
        s = _dot_nt(q_ref[0, row0:row0 + n_rows, lanes], k_ref[0, pl.ds(key0, n_keys), lanes])
        cq = cqs[hh][row0:row0 + n_rows]
        ck = c_ref[0, g, hh % 2:hh % 2 + 1, pl.ds(key0, n_keys)]
        return jnp.concatenate([c + cq - ck[:, n * LANES:(n + 1) * LANES]
                                for n, c in enumerate(_lane_chunks(s))], axis=1)

    outs = _attend(blk, score_fn, _pair_values(vlo_ref, vhi_ref), (s0_ref, s1_ref), m_ref, acc_ref)
    for g, out in enumerate(outs):
        o_ref[0, :, g * LANES:(g + 1) * LANES] = out.astype(BF16)


def _fox_attention(q_lo, q_hi, k, v_lo, v_hi, c_t):
    b, s, w = k.shape
    pair_rows, pair_seq = _attention_specs(s, FOX_PAIRS * LANES)
    gates = pl.BlockSpec((1, FOX_PAIRS, 2, s), lambda bi, hg, qi: (bi, hg, 0, 0))
    return pl.pallas_call(
        _fox_body,
        grid=(b, C_HEADS // (2 * FOX_PAIRS), s // ATT_TILE),
        in_specs=[pair_rows, pair_rows, pair_seq, pair_seq, pair_seq, gates],
        out_specs=pair_rows,
        out_shape=jax.ShapeDtypeStruct((b, s, w), BF16),
        scratch_shapes=_attend_scratch(FOX_PAIRS),
        compiler_params=_params("parallel", "parallel", "arbitrary"),
        name="fox_attention",
    )(q_lo, q_hi, k, v_lo, v_hi, c_t)


def _swa_body(sink_ref, q_ref, k_ref, vlo_ref, vhi_ref, bias_ref, o_ref):
    w = WINDOW
    n_sub = SWA_TQ // w
    n_groups = A_HEADS // 2
    v_refs = (vlo_ref, vhi_ref)
    for sb in range(n_sub):
        blk = pl.program_id(1) * n_sub + sb
        k_start = pl.multiple_of(jnp.maximum(blk - 1, 0) * w, w)
        table = jnp.minimum(blk, 1)
        k_tile = k_ref[0, pl.ds(k_start, 2 * w), :]
        heads = [(grp, half) for grp in range(n_groups) for half in range(2)]
        scores = []
        for grp, half in heads:
            q_half = _split_pair(q_ref[0, sb * w:(sb + 1) * w, grp * LANES:(grp + 1) * LANES])[half]
            scores.append(_dot_nt(q_half, k_tile) + bias_ref[table, grp + n_groups * half])
        probs = []
        for (grp, half), s in zip(heads, scores):
            chunks = _lane_chunks(s)
            sink = sink_ref[grp + n_groups * half]
            row_max = jnp.max(functools.reduce(jnp.maximum, chunks), axis=-1, keepdims=True)
            m = jnp.maximum(jnp.broadcast_to(row_max, (w, LANES)), sink)
            e = jnp.concatenate([jnp.exp2(c - m).astype(BF16) for c in chunks], axis=1)
            probs.append((e, jnp.exp2(sink - m)))
        out = []
        for (grp, half), (e, sink_term) in zip(heads, probs):
            pv = _dot(e, v_refs[half][0, pl.ds(k_start, 2 * w), :])
            out.append(pv / (pltpu.roll(pv, HALF, axis=1) + sink_term))
        for grp in range(n_groups):
            o_ref[0, sb * w:(sb + 1) * w, grp * LANES:(grp + 1) * LANES] = jnp.where(
                _lane_halves(out[0].shape), out[2 * grp], out[2 * grp + 1]).astype(BF16)


def _swa_bias_tables():
    w = WINDOW
    row = lax.broadcasted_iota(jnp.int32, (w, 2 * w), 0)
    col = lax.broadcasted_iota(jnp.int32, (w, 2 * w), 1)
    slopes = 2.0 ** (-8.0 * jnp.arange(1, A_HEADS + 1, dtype=F32) / A_HEADS)
    tables = []
    for dist in (row - col, row + w - col):
        band = (dist >= 0) & (dist < w)
        alibi = -LOG2E * slopes[:, None, None] * dist.astype(F32)[None]
        tables.append(jnp.where(band[None], alibi, -jnp.inf))
    return jnp.stack(tables)


def _swa_attention(q, k, v_lo, v_hi, sinks):
    b, s, wq = q.shape
    tq = SWA_TQ
    bias = _swa_bias_tables()
    rows = pl.BlockSpec((1, tq, wq), lambda bi, qi: (bi, qi, 0))
    seq = pl.BlockSpec((1, s, LANES), lambda bi, qi: (bi, 0, 0))
    return pl.pallas_call(
        _swa_body,
        grid=(b, s // tq),
        in_specs=[pl.BlockSpec(memory_space=pltpu.SMEM), rows, seq, seq, seq, _resident(bias.shape)],
        out_specs=rows,
        out_shape=jax.ShapeDtypeStruct((b, s, wq), BF16),
        compiler_params=_params("parallel", "arbitrary"),
        name="swa_attention",
    )(sinks, q, k, v_lo, v_hi, bias)


def _swa_head_perm():
    n_groups = A_HEADS // 2
    idx = []
    for grp in range(n_groups):
        for head in (grp, grp + n_groups):
            idx.extend(range(head * A_HEAD_DIM, (head + 1) * A_HEAD_DIM))
    return np.asarray(idx, np.int32)


def _pad_cols(w, width):
    return jnp.pad(w, ((0, 0), (0, width - w.shape[1])))


def _even_weights(w_in, w_uq, w_ukv, w_out):
    perm = _swa_head_perm()
    nq = A_HEADS * A_HEAD_DIM
    nkv = A_KV_HEADS * A_HEAD_DIM
    o_cq = nq + 2 * nkv
    o_ckv = o_cq + B_Q_LORA
    o_kr = o_ckv + B_KV_LORA
    aq = w_in[:, :nq][:, perm] * (A_HEAD_DIM ** -0.5)
    kr = w_in[:, o_kr:o_kr + B_ROPE_DIM]
    kr_placed = jnp.pad(kr, ((0, 0), (B_NOPE_DIM, LANES - B_NOPE_DIM - B_ROPE_DIM)))
    win = jnp.concatenate([aq, w_in[:, nq:o_kr], kr_placed], axis=1).astype(BF16)

    half = B_ROPE_DIM // 2
    uq = w_uq.reshape(B_Q_LORA, B_HEADS, B_NOPE_DIM + B_ROPE_DIM)
    uq_swap = jnp.concatenate([jnp.zeros_like(uq[..., :B_NOPE_DIM]),
                               uq[..., B_NOPE_DIM + half:], uq[..., B_NOPE_DIM:B_NOPE_DIM + half]], axis=-1)
    pad_head = lambda x: jnp.pad(x, ((0, 0), (0, 0), (0, LANES - x.shape[-1]))).reshape(x.shape[0], B_HEADS * LANES)
    wuq = pad_head(uq).astype(BF16)
    wuqs = pad_head(uq_swap).astype(BF16)

    ukv = w_ukv.reshape(B_KV_LORA, B_HEADS, B_NOPE_DIM + B_V_DIM)
    wk = pad_head(ukv[..., :B_NOPE_DIM]).astype(BF16)
    wv = ukv[..., B_NOPE_DIM:].reshape(B_KV_LORA, B_HEADS * B_V_DIM).astype(BF16)

    wo_a = w_out[:nq][perm].astype(BF16)
    wo_b = w_out[nq:].astype(BF16)
    return win, wuq, wuqs, wk, wv, wo_a, wo_b


def _rope_tables(seq):
    inv = ROPE_THETA ** (-jnp.arange(0, B_ROPE_DIM, 2, dtype=F32) / B_ROPE_DIM)
    ang = jnp.arange(seq, dtype=F32)[:, None] * inv[None, :]
    cos, sin = jnp.cos(ang), jnp.sin(ang)
    ones = jnp.ones((seq, B_NOPE_DIM), F32)
    zeros = jnp.zeros((seq, B_NOPE_DIM), F32)
    tail = jnp.zeros((seq, LANES - B_NOPE_DIM - B_ROPE_DIM), F32)
    c_tab = jnp.concatenate([ones, cos, cos, tail], axis=1)
    s_tab = jnp.concatenate([zeros, -sin, sin, tail], axis=1)
    scale = (B_NOPE_DIM + B_ROPE_DIM) ** -0.5 * LOG2E
    return c_tab * scale, s_tab * scale, c_tab, s_tab


def _odd_weights(w_in, b_f):
    w = C_HEADS * C_HEAD_DIM
    win = jnp.concatenate([w_in[:, :3 * w], _pad_cols(w_in[:, 3 * w:], LANES)], axis=1).astype(BF16)
    bf = _pad_cols(b_f.reshape(1, C_HEADS).astype(F32), LANES)
    return win, bf


def kernel(x, p, ffa_norm, ffa_w_gate_up, ffa_w_down, mix_norm, ffb_norm, ffb_w_gate_up, ffb_w_down, ple_norm, ple_w_gate, ple_w_proj, ev_w_in, ev_sinks, ev_cq_norm, ev_w_uq, ev_ckv_norm, ev_w_ukv, ev_w_out, od_w_in, od_b_f, od_w_out, final_norm):
    batch, seq, d = x.shape
    depth = p.shape[0]
    t = batch * seq
    h = x.reshape(t, d)
    p_all = p.reshape(depth, t, PLE_DIM)
    vec = lambda a: a.reshape(1, -1).astype(F32)
    vecs = lambda a: a.reshape(a.shape[0], 1, a.shape[1]).astype(F32)
    tabs = _rope_tables(seq)
    ga, gm, gb, gp = vecs(ffa_norm), vecs(mix_norm), vecs(ffb_norm), vecs(ple_norm)
    wa_gu, wa_d = ffa_w_gate_up.astype(BF16), ffa_w_down.astype(BF16)
    wb_gu, wb_d = ffb_w_gate_up.astype(BF16), ffb_w_down.astype(BF16)
    wg, wp = ple_w_gate.astype(BF16), ple_w_proj.astype(BF16)
    gf = vec(final_norm)

    for i in range(depth):
        j = i // 2
        if i % 2 == 0:
            win, wuq, wuqs, wk, wv, wo_a, wo_b = _even_weights(ev_w_in[j], ev_w_uq[j], ev_w_ukv[j], ev_w_out[j])
            small = (win, vec(ev_cq_norm[j]), wuq, wuqs, vec(ev_ckv_norm[j]), wk, wv)
            h, aq, ak, av_lo, av_hi, mq, mk, mv_lo, mv_hi = _pre_even(h, i, ga, wa_gu, wa_d, gm, small, tabs, seq)
            b3 = lambda a: a.reshape(batch, seq, a.shape[-1])
            out_a = _swa_attention(b3(aq), b3(ak), b3(av_lo), b3(av_hi), ev_sinks[j].astype(F32) * LOG2E)
            out_b = _mla_attention(b3(mq), b3(mk), b3(mv_lo), b3(mv_hi))
            xs, wos = [out_a.reshape(t, -1), out_b.reshape(t, -1)], [wo_a, wo_b]
        else:
            win, bf = _odd_weights(od_w_in[j], od_b_f[j])
            h3, q_lo, q_hi, k, v_lo, v_hi, c_t = _pre_odd(h.reshape(batch, seq, d), i, ga, wa_gu, wa_d, gm, win, bf)
            h = h3.reshape(t, d)
            out = _fox_attention(q_lo, q_hi, k, v_lo, v_hi, c_t.reshape(batch, C_HEADS // 2, 2, seq))
            xs, wos = [out.reshape(t, -1)], [od_w_out[j].astype(BF16)]
        h = _post(h, xs, wos, i, gb, wb_gu, wb_d, p_all, gp, wg, wp, gf, final=(i == depth - 1))
    return h.reshape(batch, seq, d)
```

```python
import functools
import math

import numpy as np
import jax
import jax.numpy as jnp
from jax import lax
from jax.experimental import pallas as pl
from jax.experimental.pallas import tpu as pltpu

F32 = jnp.float32
BF16 = jnp.bfloat16

D_MODEL = 1024
D_FF = 2816
FFN_RES_SCALE = 0.5
RMS_EPS = 1e-6
PLE_DIM = 256
A_HEADS = 8
A_KV_HEADS = 2
A_HEAD_DIM = 64
WINDOW = 128
B_HEADS = 8
B_Q_LORA = 256
B_KV_LORA = 128
B_NOPE_DIM = 64
B_ROPE_DIM = 32
B_V_DIM = 64
ROPE_THETA = 10000.0
C_HEADS = 16
C_HEAD_DIM = 64

LOG2E = math.log2(math.e)
LANES = 128
HALF = LANES // 2
VMEM_LIMIT = 64 * 1024 * 1024

TOKEN_TILE = 512
POST_TILE = 1024
FF_CHUNK = 256
ATT_TILE = 1024
MLA_PAIRS = 1
FOX_PAIRS = 2
ATT_DIAG_STRIPS = 4
SWA_TQ = 512


def _resident(shape):
    nd = len(shape)
    return pl.BlockSpec(shape, lambda *_: (0,) * nd, pipeline_mode=pl.Buffered(1))


def _layer_resident(stack, layer):
    _, a, b = stack.shape
    return pl.BlockSpec((None, a, b), lambda *_: (layer, 0, 0), pipeline_mode=pl.Buffered(1))


def _params(*sem):
    return pltpu.CompilerParams(dimension_semantics=sem, vmem_limit_bytes=VMEM_LIMIT)


def _rms(x, g):
    ms = jnp.mean(x * x, axis=-1, keepdims=True)
    return x * lax.rsqrt(ms + RMS_EPS) * g


def _dot(a, b):
    return jnp.dot(a, b, preferred_element_type=F32)


def _dot_nt(a, b):
    return lax.dot_general(a, b, (((1,), (1,)), ((), ())), preferred_element_type=F32)


def _lane_halves(shape):
    lane = lax.broadcasted_iota(jnp.int32, shape, len(shape) - 1)
    return lane < HALF


def _pair_halves(x, fill):
    lo = _lane_halves(x.shape)
    return jnp.where(lo, x, fill).astype(BF16), jnp.where(lo, fill, x).astype(BF16)


def _with_ones(v):
    return _pair_halves(v, 1.0)


def _ffn_half_step(x, g_ref, wgu_ref, wd_ref, act_ref):
    xn = _rms(x, g_ref[...]).astype(BF16)
    for c in range(D_FF // FF_CHUNK):
        lo = c * FF_CHUNK
        gate = _dot(xn, wgu_ref[:, lo:lo + FF_CHUNK])
        up = _dot(xn, wgu_ref[:, D_FF + lo:D_FF + lo + FF_CHUNK])
        act_ref[:, lo:lo + FF_CHUNK] = (gate * jax.nn.sigmoid(gate) * up).astype(BF16)
    return x + FFN_RES_SCALE * _dot(act_ref[...], wd_ref[...])


def _pre_even_body(h_ref, ga_ref, wgu_ref, wd_ref, gm_ref, win_ref, cqn_ref, wuq_ref, wuqs_ref,
                   ckvn_ref, wk_ref, wv_ref, cq_ref, sq_ref, ck_ref, sk_ref,
                   h1_ref, aq_ref, ak_ref, avlo_ref, avhi_ref, mq_ref, mk_ref, mvlo_ref, mvhi_ref, act_ref):
    y = _ffn_half_step(h_ref[...], ga_ref, wgu_ref, wd_ref, act_ref)
    h1_ref[...] = y
    xn = _rms(y, gm_ref[...]).astype(BF16)
    z = _dot(xn, win_ref[...])
    o_ak = A_HEADS * A_HEAD_DIM
    o_av = o_ak + LANES
    o_cq = o_av + LANES
    o_ckv = o_cq + B_Q_LORA
    o_kr = o_ckv + B_KV_LORA
    o_krs = o_kr + LANES
    aq_ref[...] = (z[:, :o_ak] * LOG2E).astype(BF16)
    ak_ref[...] = z[:, o_ak:o_av].astype(BF16)
    avlo_ref[...], avhi_ref[...] = _with_ones(z[:, o_av:o_cq])
    cqn = _rms(z[:, o_cq:o_ckv], cqn_ref[...]).astype(BF16)
    ckvn = _rms(z[:, o_ckv:o_kr], ckvn_ref[...]).astype(BF16)
    k_rope = z[:, o_kr:o_krs] * ck_ref[...] + z[:, o_krs:o_krs + LANES] * sk_ref[...]
    q_plain = _dot(cqn, wuq_ref[...])
    q_swap = _dot(cqn, wuqs_ref[...])
    k_nope = _dot(ckvn, wk_ref[...])
    cq = cq_ref[...]
    sq = sq_ref[...]
    for h in range(B_HEADS):
        sl = slice(h * LANES, (h + 1) * LANES)
        mq_ref[:, sl] = (q_plain[:, sl] * cq + q_swap[:, sl] * sq).astype(BF16)
        mk_ref[:, sl] = (k_nope[:, sl] + k_rope).astype(BF16)
    v = _dot(ckvn, wv_ref[...])
    for grp in range(B_HEADS // 2):
        sl = slice(grp * LANES, (grp + 1) * LANES)
        mvlo_ref[:, sl], mvhi_ref[:, sl] = _with_ones(v[:, sl])


def _pre_even(h, layer, ga, wgu, wd, gm, small, tabs, seq):
    t, d = h.shape
    tm = TOKEN_TILE
    n_pos = seq // tm
    row = lambda w: pl.BlockSpec((tm, w), lambda i: (i, 0))
    tab = pl.BlockSpec((tm, LANES), lambda i: (i % n_pos, 0))
    widths = (A_HEADS * A_HEAD_DIM, LANES, LANES, LANES, B_HEADS * LANES, B_HEADS * LANES,
              B_HEADS * B_V_DIM, B_HEADS * B_V_DIM)
    stacks = (ga, wgu, wd, gm)
    return pl.pallas_call(
        _pre_even_body,
        grid=(t // tm,),
        in_specs=[row(d)] + [_layer_resident(a, layer) for a in stacks]
        + [_resident(a.shape) for a in small] + [tab] * 4,
        out_specs=[row(d)] + [row(w) for w in widths],
        out_shape=[jax.ShapeDtypeStruct((t, d), F32)] + [jax.ShapeDtypeStruct((t, w), BF16) for w in widths],
        scratch_shapes=[pltpu.VMEM((tm, D_FF), BF16)],
        compiler_params=_params("parallel"),
        name="pre_even",
    )(h, *stacks, *small, *tabs)


def _pre_odd_body(h_ref, ga_ref, wgu_ref, wd_ref, gm_ref, win_ref, bf_ref,
                  h1_ref, qlo_ref, qhi_ref, k_ref, vlo_ref, vhi_ref, ct_ref, act_ref, carry_ref):
    @pl.when(pl.program_id(1) == 0)
    def _():
        carry_ref[...] = jnp.zeros_like(carry_ref)

    y = _ffn_half_step(h_ref[0], ga_ref, wgu_ref, wd_ref, act_ref)
    h1_ref[0] = y
    xn = _rms(y, gm_ref[...]).astype(BF16)
    w = C_HEADS * C_HEAD_DIM
    q = _dot(xn, win_ref[:, 0:w]) * (C_HEAD_DIM ** -0.5 * LOG2E)
    k_ref[0] = _dot(xn, win_ref[:, w:2 * w]).astype(BF16)
    v = _dot(xn, win_ref[:, 2 * w:3 * w])
    for grp in range(C_HEADS // 2):
        sl = slice(grp * LANES, (grp + 1) * LANES)
        qlo_ref[0, :, sl], qhi_ref[0, :, sl] = _pair_halves(q[:, sl], 0.0)
        vlo_ref[0, :, sl], vhi_ref[0, :, sl] = _with_ones(v[:, sl])
    f = _dot(xn, win_ref[:, 3 * w:3 * w + LANES]) + bf_ref[...]
    c = jnp.minimum(f, 0.0) - jnp.log1p(jnp.exp(-jnp.abs(f)))
    tm = c.shape[0]
    rows = lax.broadcasted_iota(jnp.int32, c.shape, 0)
    shift = 1
    while shift < tm:
        c = c + jnp.where(rows >= shift, pltpu.roll(c, shift, axis=0), 0.0)
        shift *= 2
    c = c + carry_ref[...]
    carry_ref[...] = c[tm - 1:tm, :]
    ct_ref[0] = jnp.transpose(c * LOG2E)[:C_HEADS, :]


def _pre_odd(h3, layer, ga, wgu, wd, gm, win, bf):
    b, s, d = h3.shape
    tm = TOKEN_TILE
    w = C_HEADS * C_HEAD_DIM
    blk = lambda n: pl.BlockSpec((1, tm, n), lambda bi, si: (bi, si, 0))
    stacks = (ga, wgu, wd, gm)
    return pl.pallas_call(
        _pre_odd_body,
        grid=(b, s // tm),
        in_specs=[blk(d)] + [_layer_resident(a, layer) for a in stacks] + [_resident(win.shape), _resident(bf.shape)],
        out_specs=[blk(d)] + [blk(w)] * 5 + [pl.BlockSpec((1, C_HEADS, tm), lambda bi, si: (bi, 0, si))],
        out_shape=[jax.ShapeDtypeStruct((b, s, d), F32)] + [jax.ShapeDtypeStruct((b, s, w), BF16)] * 5
        + [jax.ShapeDtypeStruct((b, C_HEADS, s), F32)],
        scratch_shapes=[pltpu.VMEM((tm, D_FF), BF16), pltpu.VMEM((1, LANES), F32)],
        compiler_params=_params("parallel", "arbitrary"),
        name="pre_odd",
    )(h3, *stacks, win, bf)


def _post_body(*refs, n_in, final):
    h_ref = refs[0]
    x_refs = refs[1:1 + n_in]
    wo_refs = refs[1 + n_in:1 + 2 * n_in]
    gb_ref, wgu_ref, wd_ref, p_ref, gp_ref, wg_ref, wp_ref, gf_ref, o_ref, act_ref = refs[1 + 2 * n_in:]
    y = h_ref[...]
    for x_ref, wo_ref in zip(x_refs, wo_refs):
        y = y + _dot(x_ref[...], wo_ref[...])
    y = _ffn_half_step(y, gb_ref, wgu_ref, wd_ref, act_ref)
    gate = jax.nn.sigmoid(_dot(_rms(y, gp_ref[...]).astype(BF16), wg_ref[...]))
    y = y + gate * _dot(p_ref[...].astype(BF16), wp_ref[...])
    if final:
        y = _rms(y, gf_ref[...])
    o_ref[...] = y


def _post(h, xs, wos, layer, gb, wgu, wd, p_all, gp, wg, wp, gf, final):
    t, d = h.shape
    tm = POST_TILE
    row = lambda w: pl.BlockSpec((tm, w), lambda i: (i, 0))
    return pl.pallas_call(
        functools.partial(_post_body, n_in=len(xs), final=final),
        grid=(t // tm,),
        in_specs=[row(d)] + [row(x.shape[1]) for x in xs] + [_resident(w.shape) for w in wos]
        + [_layer_resident(a, layer) for a in (gb, wgu, wd)]
        + [pl.BlockSpec((None, tm, PLE_DIM), lambda i: (layer, i, 0))]
        + [_layer_resident(a, layer) for a in (gp, wg, wp)] + [_resident(gf.shape)],
        out_specs=row(d),
        out_shape=jax.ShapeDtypeStruct((t, d), F32),
        scratch_shapes=[pltpu.VMEM((tm, D_FF), BF16)],
        compiler_params=_params("parallel"),
        name="post",
    )(h, *xs, *wos, gb, wgu, wd, p_all, gp, wg, wp, gf)


def _split_pair(q):
    lo = _lane_halves(q.shape)
    qf = q.astype(F32)
    return jnp.where(lo, qf, 0.0).astype(BF16), jnp.where(lo, 0.0, qf).astype(BF16)


def _lane_chunks(s):
    return [s[:, c * LANES:(c + 1) * LANES] for c in range(s.shape[1] // LANES)]


def _online_step(chunks, v_ones, state):
    m_prev, acc = state
    m_new = jnp.maximum(m_prev, jnp.max(functools.reduce(jnp.maximum, chunks), axis=-1, keepdims=True))
    alpha = jnp.exp2(m_prev - m_new)
    p = jnp.concatenate([jnp.exp2(c - m_new).astype(BF16) for c in chunks], axis=1)
    return m_new, alpha * acc + _dot(p, v_ones)


def _merge_pair(acc_lo, acc_hi):
    n0 = acc_lo / pltpu.roll(acc_lo, HALF, axis=1)
    n1 = acc_hi / pltpu.roll(acc_hi, HALF, axis=1)
    return jnp.where(_lane_halves(acc_lo.shape), n0, n1)


def _attend(blk, score_fn, v_fn, s_refs, m_ref, acc_ref):
    t = ATT_TILE
    heads = range(m_ref.shape[0])
    m_ref[...] = jnp.full(m_ref.shape, -jnp.inf, F32)
    acc_ref[...] = jnp.zeros(acc_ref.shape, F32)

    def scores(j, s_ref):
        off = pl.multiple_of(j * t, t)
        for hh in heads:
            s_ref[hh] = score_fn(hh, 0, t, off, t)

    def consume(j, s_ref):
        off = pl.multiple_of(j * t, t)
        for hh in heads:
            m_ref[hh], acc_ref[hh] = _online_step(_lane_chunks(s_ref[hh]), v_fn(hh, off, t),
                                                  (m_ref[hh], acc_ref[hh]))

    strip = t // ATT_DIAG_STRIPS
    row_strips = tuple((r * strip, (r + 1) * strip) for r in range(ATT_DIAG_STRIPS))

    def scores_diagonal(s_ref):
        off = pl.multiple_of(blk * t, t)
        for hh in heads:
            for row0, n_keys in row_strips:
                s_ref[hh, row0:row0 + strip, 0:n_keys] = score_fn(hh, row0, strip, off, n_keys)

    def consume_diagonal(s_ref):
        off = pl.multiple_of(blk * t, t)
        tri = (lax.broadcasted_iota(jnp.int32, (strip, strip), 1)
               <= lax.broadcasted_iota(jnp.int32, (strip, strip), 0))
        for hh in heads:
            for row0, n_keys in row_strips:
                rows = slice(row0, row0 + strip)
                square = jnp.where(tri, s_ref[hh, rows, n_keys - strip:n_keys], -jnp.inf)
                chunks = _lane_chunks(s_ref[hh, rows, 0:n_keys - strip]) + _lane_chunks(square)
                m_ref[hh, rows], acc_ref[hh, rows] = _online_step(chunks, v_fn(hh, off, n_keys),
                                                                  (m_ref[hh, rows], acc_ref[hh, rows]))

    s_even, s_odd = s_refs
    scores(0, s_even)

    def two_tiles(pair, carry):
        j = 2 * pair
        scores(j + 1, s_odd)
        consume(j, s_even)
        scores(j + 2, s_even)
        consume(j + 1, s_odd)
        return carry

    lax.fori_loop(0, blk // 2, two_tiles, 0)

    @pl.when(blk % 2 == 1)
    def _():
        scores_diagonal(s_odd)
        consume(blk - 1, s_even)
        consume_diagonal(s_odd)

    @pl.when(blk % 2 == 0)
    def _():
        consume_diagonal(s_even)

    return [_merge_pair(acc_ref[2 * g], acc_ref[2 * g + 1]) for g in range(len(heads) // 2)]


def _attend_scratch(n_pairs):
    t, n = ATT_TILE, 2 * n_pairs
    return [pltpu.VMEM((n, t, t), F32), pltpu.VMEM((n, t, t), F32),
            pltpu.VMEM((n, t, LANES), F32), pltpu.VMEM((n, t, LANES), F32)]


def _pair_values(vlo_ref, vhi_ref):
    def v_fn(hh, key0, n_keys):
        g, ref = hh // 2, (vlo_ref, vhi_ref)[hh % 2]
        return ref[0, pl.ds(key0, n_keys), g * LANES:(g + 1) * LANES]
    return v_fn


def _mla_body(q_ref, k_ref, vlo_ref, vhi_ref, o_ref, s0_ref, s1_ref, m_ref, acc_ref):
    def score_fn(hh, row0, n_rows, key0, n_keys):
        sl = slice(hh * LANES, (hh + 1) * LANES)
        return _dot_nt(q_ref[0, row0:row0 + n_rows, sl], k_ref[0, pl.ds(key0, n_keys), sl])

    outs = _attend(pl.program_id(2), score_fn, _pair_values(vlo_ref, vhi_ref), (s0_ref, s1_ref), m_ref, acc_ref)
    for g, out in enumerate(outs):
        o_ref[0, :, g * LANES:(g + 1) * LANES] = out.astype(BF16)


def _attention_specs(s, width):
    rows = pl.BlockSpec((1, ATT_TILE, width), lambda bi, hg, qi: (bi, qi, hg))
    seq = pl.BlockSpec((1, s, width), lambda bi, hg, qi: (bi, 0, hg))
    return rows, seq


def _mla_attention(q, k, v_lo, v_hi):
    b, s, _ = q.shape
    head_rows, head_seq = _attention_specs(s, MLA_PAIRS * 2 * LANES)
    pair_rows, pair_seq = _attention_specs(s, MLA_PAIRS * LANES)
    return pl.pallas_call(
        _mla_body,
        grid=(b, B_HEADS // (2 * MLA_PAIRS), s // ATT_TILE),
        in_specs=[head_rows, head_seq, pair_seq, pair_seq],
        out_specs=pair_rows,
        out_shape=jax.ShapeDtypeStruct((b, s, B_HEADS * B_V_DIM), BF16),
        scratch_shapes=_attend_scratch(MLA_PAIRS),
        compiler_params=_params("parallel", "parallel", "arbitrary"),
        name="mla_attention",
    )(q, k, v_lo, v_hi)


def _fox_body(qlo_ref, qhi_ref, k_ref, vlo_ref, vhi_ref, c_ref, o_ref, s0_ref, s1_ref, m_ref, acc_ref):
    t = ATT_TILE
    blk = pl.program_id(2)
    rows = pl.ds(pl.multiple_of(blk * t, t), t)
    cqs = [jnp.transpose(jnp.broadcast_to(c_ref[0, hh // 2, hh % 2:hh % 2 + 1, rows], (LANES, t)))
           for hh in range(2 * FOX_PAIRS)]

    def score_fn(hh, row0, n_rows, key0, n_keys):
        g, q_ref = hh // 2, (qlo_ref, qhi_ref)[hh % 2]
        lanes = slice(g * LANES, (g + 1) * LANES)
        s = _dot_nt(q_ref[0, row0:row0 + n_rows, lanes], k_ref[0, pl.ds(key0, n_keys), lanes])
        cq = cqs[hh][row0:row0 + n_rows]
        ck = c_ref[0, g, hh % 2:hh % 2 + 1, pl.ds(key0, n_keys)]
        return jnp.concatenate([c + cq - ck[:, n * LANES:(n + 1) * LANES]
                                for n, c in enumerate(_lane_chunks(s))], axis=1)

    outs = _attend(blk, score_fn, _pair_values(vlo_ref, vhi_ref), (s0_ref, s1_ref), m_ref, acc_ref)
    for g, out in enumerate(outs):
        o_ref[0, :, g * LANES:(g + 1) * LANES] = out.astype(BF16)


def _fox_attention(q_lo, q_hi, k, v_lo, v_hi, c_t):
    b, s, w = k.shape
    pair_rows, pair_seq = _attention_specs(s, FOX_PAIRS * LANES)
    gates = pl.BlockSpec((1, FOX_PAIRS, 2, s), lambda bi, hg, qi: (bi, hg, 0, 0))
    return pl.pallas_call(
        _fox_body,
        grid=(b, C_HEADS // (2 * FOX_PAIRS), s // ATT_TILE),
        in_specs=[pair_rows, pair_rows, pair_seq, pair_seq, pair_seq, gates],
        out_specs=pair_rows,
        out_shape=jax.ShapeDtypeStruct((b, s, w), BF16),
        scratch_shapes=_attend_scratch(FOX_PAIRS),
        compiler_params=_params("parallel", "parallel", "arbitrary"),
        name="fox_attention",
    )(q_lo, q_hi, k, v_lo, v_hi, c_t)


def _swa_body(sink_ref, q_ref, k_ref, vlo_ref, vhi_ref, bias_ref, o_ref):
    w = WINDOW
    n_sub = SWA_TQ // w
    n_groups = A_HEADS // 2
    v_refs = (vlo_ref, vhi_ref)
    for sb in range(n_sub):
        blk = pl.program_id(1) * n_sub + sb
        k_start = pl.multiple_of(jnp.maximum(blk - 1, 0) * w, w)
        table = jnp.minimum(blk, 1)
        k_tile = k_ref[0, pl.ds(k_start, 2 * w), :]
        heads = [(grp, half) for grp in range(n_groups) for half in range(2)]
        scores = []
        for grp, half in heads:
            q_half = _split_pair(q_ref[0, sb * w:(sb + 1) * w, grp * LANES:(grp + 1) * LANES])[half]
            scores.append(_dot_nt(q_half, k_tile) + bias_ref[table, grp + n_groups * half])
        probs = []
        for (grp, half), s in zip(heads, scores):
            chunks = _lane_chunks(s)
            sink = sink_ref[grp + n_groups * half]
            row_max = jnp.max(functools.reduce(jnp.maximum, chunks), axis=-1, keepdims=True)
            m = jnp.maximum(jnp.broadcast_to(row_max, (w, LANES)), sink)
            e = jnp.concatenate([jnp.exp2(c - m).astype(BF16) for c in chunks], axis=1)
            probs.append((e, jnp.exp2(sink - m)))
        out = []
        for (grp, half), (e, sink_term) in zip(heads, probs):
            pv = _dot(e, v_refs[half][0, pl.ds(k_start, 2 * w), :])
            out.append(pv / (pltpu.roll(pv, HALF, axis=1) + sink_term))
        for grp in range(n_groups):
            o_ref[0, sb * w:(sb + 1) * w, grp * LANES:(grp + 1) * LANES] = jnp.where(
                _lane_halves(out[0].shape), out[2 * grp], out[2 * grp + 1]).astype(BF16)


def _swa_bias_tables():
    w = WINDOW
    row = lax.broadcasted_iota(jnp.int32, (w, 2 * w), 0)
    col = lax.broadcasted_iota(jnp.int32, (w, 2 * w), 1)
    slopes = 2.0 ** (-8.0 * jnp.arange(1, A_HEADS + 1, dtype=F32) / A_HEADS)
    tables = []
    for dist in (row - col, row + w - col):
        band = (dist >= 0) & (dist < w)
        alibi = -LOG2E * slopes[:, None, None] * dist.astype(F32)[None]
        tables.append(jnp.where(band[None], alibi, -jnp.inf))
    return jnp.stack(tables)


def _swa_attention(q, k, v_lo, v_hi, sinks):
    b, s, wq = q.shape
    tq = SWA_TQ
    bias = _swa_bias_tables()
    rows = pl.BlockSpec((1, tq, wq), lambda bi, qi: (bi, qi, 0))
    seq = pl.BlockSpec((1, s, LANES), lambda bi, qi: (bi, 0, 0))
    return pl.pallas_call(
        _swa_body,
        grid=(b, s // tq),
        in_specs=[pl.BlockSpec(memory_space=pltpu.SMEM), rows, seq, seq, seq, _resident(bias.shape)],
        out_specs=rows,
        out_shape=jax.ShapeDtypeStruct((b, s, wq), BF16),
        compiler_params=_params("parallel", "arbitrary"),
        name="swa_attention",
    )(sinks, q, k, v_lo, v_hi, bias)


def _swa_head_perm():
    n_groups = A_HEADS // 2
    idx = []
    for grp in range(n_groups):
        for head in (grp, grp + n_groups):
            idx.extend(range(head * A_HEAD_DIM, (head + 1) * A_HEAD_DIM))
    return np.asarray(idx, np.int32)


def _pad_cols(w, width):
    return jnp.pad(w, ((0, 0), (0, width - w.shape[1])))


def _even_weights(w_in, w_uq, w_ukv, w_out):
    perm = _swa_head_perm()
    nq = A_HEADS * A_HEAD_DIM
    nkv = A_KV_HEADS * A_HEAD_DIM
    o_cq = nq + 2 * nkv
    o_ckv = o_cq + B_Q_LORA
    o_kr = o_ckv + B_KV_LORA
    half = B_ROPE_DIM // 2
    aq = w_in[:, :nq][:, perm] * (A_HEAD_DIM ** -0.5)
    kr = w_in[:, o_kr:o_kr + B_ROPE_DIM]
    kr_swap = jnp.concatenate([kr[:, half:], kr[:, :half]], axis=1)
    place = lambda x: jnp.pad(x, ((0, 0), (B_NOPE_DIM, LANES - B_NOPE_DIM - B_ROPE_DIM)))
    win = jnp.concatenate([aq, w_in[:, nq:o_kr], place(kr), place(kr_swap)], axis=1).astype(BF16)

    dq = B_NOPE_DIM + B_ROPE_DIM
    uq = w_uq.reshape(B_Q_LORA, B_HEADS, dq)
    uq_swap = jnp.concatenate([jnp.zeros_like(uq[..., :B_NOPE_DIM]),
                               uq[..., B_NOPE_DIM + half:], uq[..., B_NOPE_DIM:B_NOPE_DIM + half]], axis=-1)
    pad_head = lambda x: jnp.pad(x, ((0, 0), (0, 0), (0, LANES - x.shape[-1]))).reshape(x.shape[0], B_HEADS * LANES)
    wuq = pad_head(uq).astype(BF16)
    wuqs = pad_head(uq_swap).astype(BF16)

    ukv = w_ukv.reshape(B_KV_LORA, B_HEADS, B_NOPE_DIM + B_V_DIM)
    wk = pad_head(ukv[..., :B_NOPE_DIM]).astype(BF16)
    wv = ukv[..., B_NOPE_DIM:].reshape(B_KV_LORA, B_HEADS * B_V_DIM).astype(BF16)

    wo_a = w_out[:nq][perm].astype(BF16)
    wo_b = w_out[nq:].astype(BF16)
    return win, wuq, wuqs, wk, wv, wo_a, wo_b


def _rope_tables(seq):
    inv = ROPE_THETA ** (-jnp.arange(0, B_ROPE_DIM, 2, dtype=F32) / B_ROPE_DIM)
    ang = jnp.arange(seq, dtype=F32)[:, None] * inv[None, :]
    cos, sin = jnp.cos(ang), jnp.sin(ang)
    ones = jnp.ones((seq, B_NOPE_DIM), F32)
    zeros = jnp.zeros((seq, B_NOPE_DIM), F32)
    tail = jnp.zeros((seq, LANES - B_NOPE_DIM - B_ROPE_DIM), F32)
    c_tab = jnp.concatenate([ones, cos, cos, tail], axis=1)
    s_tab = jnp.concatenate([zeros, -sin, sin, tail], axis=1)
    scale = (B_NOPE_DIM + B_ROPE_DIM) ** -0.5 * LOG2E
    return c_tab * scale, s_tab * scale, c_tab, s_tab


def _odd_weights(w_in, b_f):
    w = C_HEADS * C_HEAD_DIM
    win = jnp.concatenate([w_in[:, :3 * w], _pad_cols(w_in[:, 3 * w:], LANES)], axis=1).astype(BF16)
    bf = _pad_cols(b_f.reshape(1, C_HEADS).astype(F32), LANES)
    return win, bf


def kernel(x, p, ffa_norm, ffa_w_gate_up, ffa_w_down, mix_norm, ffb_norm, ffb_w_gate_up, ffb_w_down, ple_norm, ple_w_gate, ple_w_proj, ev_w_in, ev_sinks, ev_cq_norm, ev_w_uq, ev_ckv_norm, ev_w_ukv, ev_w_out, od_w_in, od_b_f, od_w_out, final_norm):
    batch, seq, d = x.shape
    depth = p.shape[0]
    t = batch * seq
    h = x.reshape(t, d)
    p_all = p.reshape(depth, t, PLE_DIM)
    vec = lambda a: a.reshape(1, -1).astype(F32)
    vecs = lambda a: a.reshape(a.shape[0], 1, a.shape[1]).astype(F32)
    tabs = _rope_tables(seq)
    ga, gm, gb, gp = vecs(ffa_norm), vecs(mix_norm), vecs(ffb_norm), vecs(ple_norm)
    wa_gu, wa_d = ffa_w_gate_up.astype(BF16), ffa_w_down.astype(BF16)
    wb_gu, wb_d = ffb_w_gate_up.astype(BF16), ffb_w_down.astype(BF16)
    wg, wp = ple_w_gate.astype(BF16), ple_w_proj.astype(BF16)
    gf = vec(final_norm)

    for i in range(depth):
        j = i // 2
        if i % 2 == 0:
            win, wuq, wuqs, wk, wv, wo_a, wo_b = _even_weights(ev_w_in[j], ev_w_uq[j], ev_w_ukv[j], ev_w_out[j])
            small = (win, vec(ev_cq_norm[j]), wuq, wuqs, vec(ev_ckv_norm[j]), wk, wv)
            h, aq, ak, av_lo, av_hi, mq, mk, mv_lo, mv_hi = _pre_even(h, i, ga, wa_gu, wa_d, gm, small, tabs, seq)
            b3 = lambda a: a.reshape(batch, seq, a.shape[-1])
            out_a = _swa_attention(b3(aq), b3(ak), b3(av_lo), b3(av_hi), ev_sinks[j].astype(F32) * LOG2E)
            out_b = _mla_attention(b3(mq), b3(mk), b3(mv_lo), b3(mv_hi))
            xs, wos = [out_a.reshape(t, -1), out_b.reshape(t, -1)], [wo_a, wo_b]
        else:
            win, bf = _odd_weights(od_w_in[j], od_b_f[j])
            h3, q_lo, q_hi, k, v_lo, v_hi, c_t = _pre_odd(h.reshape(batch, seq, d), i, ga, wa_gu, wa_d, gm, win, bf)
            h = h3.reshape(t, d)
            out = _fox_attention(q_lo, q_hi, k, v_lo, v_hi, c_t.reshape(batch, C_HEADS // 2, 2, seq))
            xs, wos = [out.reshape(t, -1)], [od_w_out[j].astype(BF16)]
        h = _post(h, xs, wos, i, gb, wb_gu, wb_d, p_all, gp, wg, wp, gf, final=(i == depth - 1))
    return h.reshape(batch, seq, d)
```

```python
import functools
import math

import numpy as np
import jax
import jax.numpy as jnp
from jax import lax
from jax.experimental import pallas as pl
from jax.experimental.pallas import tpu as pltpu

F32 = jnp.float32
BF16 = jnp.bfloat16

D_MODEL = 1024
D_FF = 2816
FFN_RES_SCALE = 0.5
RMS_EPS = 1e-6
PLE_DIM = 256
A_HEADS = 8
A_KV_HEADS = 2
A_HEAD_DIM = 64
WINDOW = 128
B_HEADS = 8
B_Q_LORA = 256
B_KV_LORA = 128
B_NOPE_DIM = 64
B_ROPE_DIM = 32
B_V_DIM = 64
ROPE_THETA = 10000.0
C_HEADS = 16
C_HEAD_DIM = 64

LOG2E = math.log2(math.e)
LANES = 128
HALF = LANES // 2
VMEM_LIMIT = 64 * 1024 * 1024

TOKEN_TILE = 512
POST_TILE = 1024
FF_CHUNK = 256
ATT_TILE = 1024
MLA_PAIRS = 1
FOX_PAIRS = 2
ATT_DIAG_STRIPS = 4
SWA_TQ = 512


def _resident(shape):
    nd = len(shape)
    return pl.BlockSpec(shape, lambda *_: (0,) * nd, pipeline_mode=pl.Buffered(1))


def _layer_resident(stack, layer):
    _, a, b = stack.shape
    return pl.BlockSpec((None, a, b), lambda *_: (layer, 0, 0), pipeline_mode=pl.Buffered(1))


def _params(*sem):
    return pltpu.CompilerParams(dimension_semantics=sem, vmem_limit_bytes=VMEM_LIMIT)


def _rms(x, g):
    ms = jnp.mean(x * x, axis=-1, keepdims=True)
    return x * lax.rsqrt(ms + RMS_EPS) * g


def _dot(a, b):
    return jnp.dot(a, b, preferred_element_type=F32)


def _dot_nt(a, b):
    return lax.dot_general(a, b, (((1,), (1,)), ((), ())), preferred_element_type=F32)


def _lane_halves(shape):
    lane = lax.broadcasted_iota(jnp.int32, shape, len(shape) - 1)
    return lane < HALF


def _pair_halves(x, fill):
    lo = _lane_halves(x.shape)
    return jnp.where(lo, x, fill).astype(BF16), jnp.where(lo, fill, x).astype(BF16)


def _with_ones(v):
    return _pair_halves(v, 1.0)


def _ffn_half_step(x, g_ref, wgu_ref, wd_ref, act_ref):
    xn = _rms(x, g_ref[...]).astype(BF16)
    for c in range(D_FF // FF_CHUNK):
        lo = c * FF_CHUNK
        gate = _dot(xn, wgu_ref[:, lo:lo + FF_CHUNK])
        up = _dot(xn, wgu_ref[:, D_FF + lo:D_FF + lo + FF_CHUNK])
        act_ref[:, lo:lo + FF_CHUNK] = (gate * jax.nn.sigmoid(gate) * up).astype(BF16)
    return x + FFN_RES_SCALE * _dot(act_ref[...], wd_ref[...])


def _swap_rope_halves(x):
    lane = lax.broadcasted_iota(jnp.int32, x.shape, 1)
    half = B_ROPE_DIM // 2
    take_upper = pltpu.roll(x, LANES - half, axis=1)
    take_lower = pltpu.roll(x, half, axis=1)
    return jnp.where(lane < B_NOPE_DIM + half, take_upper, take_lower)


def _pre_even_body(h_ref, ga_ref, wgu_ref, wd_ref, gm_ref, win_ref, cqn_ref, wuq_ref, wuqs_ref,
                   ckvn_ref, wk_ref, wv_ref, cq_ref, sq_ref, ck_ref, sk_ref,
                   h1_ref, aq_ref, ak_ref, avlo_ref, avhi_ref, mq_ref, mk_ref, mvlo_ref, mvhi_ref, act_ref):
    y = _ffn_half_step(h_ref[...], ga_ref, wgu_ref, wd_ref, act_ref)
    h1_ref[...] = y
    xn = _rms(y, gm_ref[...]).astype(BF16)
    z = _dot(xn, win_ref[...])
    o_ak = A_HEADS * A_HEAD_DIM
    o_av = o_ak + LANES
    o_cq = o_av + LANES
    o_ckv = o_cq + B_Q_LORA
    o_kr = o_ckv + B_KV_LORA
    aq_ref[...] = (z[:, :o_ak] * LOG2E).astype(BF16)
    ak_ref[...] = z[:, o_ak:o_av].astype(BF16)
    avlo_ref[...], avhi_ref[...] = _with_ones(z[:, o_av:o_cq])
    cqn = _rms(z[:, o_cq:o_ckv], cqn_ref[...]).astype(BF16)
    ckvn = _rms(z[:, o_ckv:o_kr], ckvn_ref[...]).astype(BF16)
    k_plain = z[:, o_kr:o_kr + LANES]
    k_rope = k_plain * ck_ref[...] + _swap_rope_halves(k_plain) * sk_ref[...]
    q_plain = _dot(cqn, wuq_ref[...])
    q_swap = _dot(cqn, wuqs_ref[...])
    k_nope = _dot(ckvn, wk_ref[...])
    cq = cq_ref[...]
    sq = sq_ref[...]
    for h in range(B_HEADS):
        sl = slice(h * LANES, (h + 1) * LANES)
        mq_ref[:, sl] = (q_plain[:, sl] * cq + q_swap[:, sl] * sq).astype(BF16)
        mk_ref[:, sl] = (k_nope[:, sl] + k_rope).astype(BF16)
    v = _dot(ckvn, wv_ref[...])
    for grp in range(B_HEADS // 2):
        sl = slice(grp * LANES, (grp + 1) * LANES)
        mvlo_ref[:, sl], mvhi_ref[:, sl] = _with_ones(v[:, sl])


def _pre_even(h, layer, ga, wgu, wd, gm, small, tabs, seq):
    t, d = h.shape
    tm = TOKEN_TILE
    n_pos = seq // tm
    row = lambda w: pl.BlockSpec((tm, w), lambda i: (i, 0))
    tab = pl.BlockSpec((tm, LANES), lambda i: (i % n_pos, 0))
    widths = (A_HEADS * A_HEAD_DIM, LANES, LANES, LANES, B_HEADS * LANES, B_HEADS * LANES,
              B_HEADS * B_V_DIM, B_HEADS * B_V_DIM)
    stacks = (ga, wgu, wd, gm)
    return pl.pallas_call(
        _pre_even_body,
        grid=(t // tm,),
        in_specs=[row(d)] + [_layer_resident(a, layer) for a in stacks]
        + [_resident(a.shape) for a in small] + [tab] * 4,
        out_specs=[row(d)] + [row(w) for w in widths],
        out_shape=[jax.ShapeDtypeStruct((t, d), F32)] + [jax.ShapeDtypeStruct((t, w), BF16) for w in widths],
        scratch_shapes=[pltpu.VMEM((tm, D_FF), BF16)],
        compiler_params=_params("parallel"),
        name="pre_even",
    )(h, *stacks, *small, *tabs)


def _pre_odd_body(h_ref, ga_ref, wgu_ref, wd_ref, gm_ref, win_ref, bf_ref,
                  h1_ref, qlo_ref, qhi_ref, k_ref, vlo_ref, vhi_ref, ct_ref, act_ref, carry_ref):
    @pl.when(pl.program_id(1) == 0)
    def _():
        carry_ref[...] = jnp.zeros_like(carry_ref)

    y = _ffn_half_step(h_ref[0], ga_ref, wgu_ref, wd_ref, act_ref)
    h1_ref[0] = y
    xn = _rms(y, gm_ref[...]).astype(BF16)
    w = C_HEADS * C_HEAD_DIM
    q = _dot(xn, win_ref[:, 0:w]) * (C_HEAD_DIM ** -0.5 * LOG2E)
    k_ref[0] = _dot(xn, win_ref[:, w:2 * w]).astype(BF16)
    v = _dot(xn, win_ref[:, 2 * w:3 * w])
    for grp in range(C_HEADS // 2):
        sl = slice(grp * LANES, (grp + 1) * LANES)
        qlo_ref[0, :, sl], qhi_ref[0, :, sl] = _pair_halves(q[:, sl], 0.0)
        vlo_ref[0, :, sl], vhi_ref[0, :, sl] = _with_ones(v[:, sl])
    f = _dot(xn, win_ref[:, 3 * w:3 * w + LANES]) + bf_ref[...]
    c = jnp.minimum(f, 0.0) - jnp.log1p(jnp.exp(-jnp.abs(f)))
    tm = c.shape[0]
    rows = lax.broadcasted_iota(jnp.int32, c.shape, 0)
    shift = 1
    while shift < tm:
        c = c + jnp.where(rows >= shift, pltpu.roll(c, shift, axis=0), 0.0)
        shift *= 2
    c = c + carry_ref[...]
    carry_ref[...] = c[tm - 1:tm, :]
    ct_ref[0] = jnp.transpose(c * LOG2E)[:C_HEADS, :]


def _pre_odd(h3, layer, ga, wgu, wd, gm, win, bf):
    b, s, d = h3.shape
    tm = TOKEN_TILE
    w = C_HEADS * C_HEAD_DIM
    blk = lambda n: pl.BlockSpec((1, tm, n), lambda bi, si: (bi, si, 0))
    stacks = (ga, wgu, wd, gm)
    return pl.pallas_call(
        _pre_odd_body,
        grid=(b, s // tm),
        in_specs=[blk(d)] + [_layer_resident(a, layer) for a in stacks] + [_resident(win.shape), _resident(bf.shape)],
        out_specs=[blk(d)] + [blk(w)] * 5 + [pl.BlockSpec((1, C_HEADS, tm), lambda bi, si: (bi, 0, si))],
        out_shape=[jax.ShapeDtypeStruct((b, s, d), F32)] + [jax.ShapeDtypeStruct((b, s, w), BF16)] * 5
        + [jax.ShapeDtypeStruct((b, C_HEADS, s), F32)],
        scratch_shapes=[pltpu.VMEM((tm, D_FF), BF16), pltpu.VMEM((1, LANES), F32)],
        compiler_params=_params("parallel", "arbitrary"),
        name="pre_odd",
    )(h3, *stacks, win, bf)


def _post_body(*refs, n_in, final):
    h_ref = refs[0]
    x_refs = refs[1:1 + n_in]
    wo_refs = refs[1 + n_in:1 + 2 * n_in]
    gb_ref, wgu_ref, wd_ref, p_ref, gp_ref, wg_ref, wp_ref, gf_ref, o_ref, act_ref = refs[1 + 2 * n_in:]
    y = h_ref[...]
    for x_ref, wo_ref in zip(x_refs, wo_refs):
        y = y + _dot(x_ref[...], wo_ref[...])
    y = _ffn_half_step(y, gb_ref, wgu_ref, wd_ref, act_ref)
    gate = jax.nn.sigmoid(_dot(_rms(y, gp_ref[...]).astype(BF16), wg_ref[...]))
    y = y + gate * _dot(p_ref[...].astype(BF16), wp_ref[...])
    if final:
        y = _rms(y, gf_ref[...])
    o_ref[...] = y


def _post(h, xs, wos, layer, gb, wgu, wd, p_all, gp, wg, wp, gf, final):
    t, d = h.shape
    tm = POST_TILE
    row = lambda w: pl.BlockSpec((tm, w), lambda i: (i, 0))
    return pl.pallas_call(
        functools.partial(_post_body, n_in=len(xs), final=final),
        grid=(t // tm,),
        in_specs=[row(d)] + [row(x.shape[1]) for x in xs] + [_resident(w.shape) for w in wos]
        + [_layer_resident(a, layer) for a in (gb, wgu, wd)]
        + [pl.BlockSpec((None, tm, PLE_DIM), lambda i: (layer, i, 0))]
        + [_layer_resident(a, layer) for a in (gp, wg, wp)] + [_resident(gf.shape)],
        out_specs=row(d),
        out_shape=jax.ShapeDtypeStruct((t, d), F32),
        scratch_shapes=[pltpu.VMEM((tm, D_FF), BF16)],
        compiler_params=_params("parallel"),
        name="post",
    )(h, *xs, *wos, gb, wgu, wd, p_all, gp, wg, wp, gf)


def _split_pair(q):
    lo = _lane_halves(q.shape)
    qf = q.astype(F32)
    return jnp.where(lo, qf, 0.0).astype(BF16), jnp.where(lo, 0.0, qf).astype(BF16)


def _lane_chunks(s):
    return [s[:, c * LANES:(c + 1) * LANES] for c in range(s.shape[1] // LANES)]


def _online_step(chunks, v_ones, state):
    m_prev, acc = state
    m_new = jnp.maximum(m_prev, jnp.max(functools.reduce(jnp.maximum, chunks), axis=-1, keepdims=True))
    alpha = jnp.exp2(m_prev - m_new)
    p = jnp.concatenate([jnp.exp2(c - m_new).astype(BF16) for c in chunks], axis=1)
    return m_new, alpha * acc + _dot(p, v_ones)


def _merge_pair(acc_lo, acc_hi):
    n0 = acc_lo / pltpu.roll(acc_lo, HALF, axis=1)
    n1 = acc_hi / pltpu.roll(acc_hi, HALF, axis=1)
    return jnp.where(_lane_halves(acc_lo.shape), n0, n1)


def _attend(blk, score_fn, v_fn, s_refs, m_ref, acc_ref):
    t = ATT_TILE
    heads = range(m_ref.shape[0])
    m_ref[...] = jnp.full(m_ref.shape, -jnp.inf, F32)
    acc_ref[...] = jnp.zeros(acc_ref.shape, F32)

    def scores(j, s_ref):
        off = pl.multiple_of(j * t, t)
        for hh in heads:
            s_ref[hh] = score_fn(hh, 0, t, off, t)

    def consume(j, s_ref):
        off = pl.multiple_of(j * t, t)
        for hh in heads:
            m_ref[hh], acc_ref[hh] = _online_step(_lane_chunks(s_ref[hh]), v_fn(hh, off, t),
                                                  (m_ref[hh], acc_ref[hh]))

    strip = t // ATT_DIAG_STRIPS
    row_strips = tuple((r * strip, (r + 1) * strip) for r in range(ATT_DIAG_STRIPS))

    def scores_diagonal(s_ref):
        off = pl.multiple_of(blk * t, t)
        for hh in heads:
            for row0, n_keys in row_strips:
                s_ref[hh, row0:row0 + strip, 0:n_keys] = score_fn(hh, row0, strip, off, n_keys)

    def consume_diagonal(s_ref):
        off = pl.multiple_of(blk * t, t)
        tri = (lax.broadcasted_iota(jnp.int32, (strip, strip), 1)
               <= lax.broadcasted_iota(jnp.int32, (strip, strip), 0))
        for hh in heads:
            for row0, n_keys in row_strips:
                rows = slice(row0, row0 + strip)
                square = jnp.where(tri, s_ref[hh, rows, n_keys - strip:n_keys], -jnp.inf)
                chunks = _lane_chunks(s_ref[hh, rows, 0:n_keys - strip]) + _lane_chunks(square)
                m_ref[hh, rows], acc_ref[hh, rows] = _online_step(chunks, v_fn(hh, off, n_keys),
                                                                  (m_ref[hh, rows], acc_ref[hh, rows]))

    s_even, s_odd = s_refs
    scores(0, s_even)

    def two_tiles(pair, carry):
        j = 2 * pair
        scores(j + 1, s_odd)
        consume(j, s_even)
        scores(j + 2, s_even)
        consume(j + 1, s_odd)
        return carry

    lax.fori_loop(0, blk // 2, two_tiles, 0)

    @pl.when(blk % 2 == 1)
    def _():
        scores_diagonal(s_odd)
        consume(blk - 1, s_even)
        consume_diagonal(s_odd)

    @pl.when(blk % 2 == 0)
    def _():
        consume_diagonal(s_even)

    return [_merge_pair(acc_ref[2 * g], acc_ref[2 * g + 1]) for g in range(len(heads) // 2)]


def _attend_scratch(n_pairs):
    t, n = ATT_TILE, 2 * n_pairs
    return [pltpu.VMEM((n, t, t), F32), pltpu.VMEM((n, t, t), F32),
            pltpu.VMEM((n, t, LANES), F32), pltpu.VMEM((n, t, LANES), F32)]


def _pair_values(vlo_ref, vhi_ref):
    def v_fn(hh, key0, n_keys):
        g, ref = hh // 2, (vlo_ref, vhi_ref)[hh % 2]
        return ref[0, pl.ds(key0, n_keys), g * LANES:(g + 1) * LANES]
    return v_fn


def _mla_body(q_ref, k_ref, vlo_ref, vhi_ref, o_ref, s0_ref, s1_ref, m_ref, acc_ref):
    def score_fn(hh, row0, n_rows, key0, n_keys):
        sl = slice(hh * LANES, (hh + 1) * LANES)
        return _dot_nt(q_ref[0, row0:row0 + n_rows, sl], k_ref[0, pl.ds(key0, n_keys), sl])

    outs = _attend(pl.program_id(2), score_fn, _pair_values(vlo_ref, vhi_ref), (s0_ref, s1_ref), m_ref, acc_ref)
    for g, out in enumerate(outs):
        o_ref[0, :, g * LANES:(g + 1) * LANES] = out.astype(BF16)


def _attention_specs(s, width):
    rows = pl.BlockSpec((1, ATT_TILE, width), lambda bi, hg, qi: (bi, qi, hg))
    seq = pl.BlockSpec((1, s, width), lambda bi, hg, qi: (bi, 0, hg))
    return rows, seq


def _mla_attention(q, k, v_lo, v_hi):
    b, s, _ = q.shape
    head_rows, head_seq = _attention_specs(s, MLA_PAIRS * 2 * LANES)
    pair_rows, pair_seq = _attention_specs(s, MLA_PAIRS * LANES)
    return pl.pallas_call(
        _mla_body,
        grid=(b, B_HEADS // (2 * MLA_PAIRS), s // ATT_TILE),
        in_specs=[head_rows, head_seq, pair_seq, pair_seq],
        out_specs=pair_rows,
        out_shape=jax.ShapeDtypeStruct((b, s, B_HEADS * B_V_DIM), BF16),
        scratch_shapes=_attend_scratch(MLA_PAIRS),
        compiler_params=_params("parallel", "parallel", "arbitrary"),
        name="mla_attention",
    )(q, k, v_lo, v_hi)


def _fox_body(qlo_ref, qhi_ref, k_ref, vlo_ref, vhi_ref, c_ref, o_ref, s0_ref, s1_ref, m_ref, acc_ref):
    t = ATT_TILE
    blk = pl.program_id(2)
    rows = pl.ds(pl.multiple_of(blk * t, t), t)
    cqs = [jnp.transpose(jnp.broadcast_to(c_ref[0, hh // 2, hh % 2:hh % 2 + 1, rows], (LANES, t)))
           for hh in range(2 * FOX_PAIRS)]

    def score_fn(hh, row0, n_rows, key0, n_keys):
        g, q_ref = hh // 2, (qlo_ref, qhi_ref)[hh % 2]
        lanes = slice(g * LANES, (g + 1) * LANES)
        s = _dot_nt(q_ref[0, row0:row0 + n_rows, lanes], k_ref[0, pl.ds(key0, n_keys), lanes])
        cq = cqs[hh][row0:row0 + n_rows]
        ck = c_ref[0, g, hh % 2:hh % 2 + 1, pl.ds(key0, n_keys)]
        return jnp.concatenate([c + cq - ck[:, n * LANES:(n + 1) * LANES]
                                for n, c in enumerate(_lane_chunks(s))], axis=1)

    outs = _attend(blk, score_fn, _pair_values(vlo_ref, vhi_ref), (s0_ref, s1_ref), m_ref, acc_ref)
    for g, out in enumerate(outs):
        o_ref[0, :, g * LANES:(g + 1) * LANES] = out.astype(BF16)


def _fox_attention(q_lo, q_hi, k, v_lo, v_hi, c_t):
    b, s, w = k.shape
    pair_rows, pair_seq = _attention_specs(s, FOX_PAIRS * LANES)
    gates = pl.BlockSpec((1, FOX_PAIRS, 2, s), lambda bi, hg, qi: (bi, hg, 0, 0))
    return pl.pallas_call(
        _fox_body,
        grid=(b, C_HEADS // (2 * FOX_PAIRS), s // ATT_TILE),
        in_specs=[pair_rows, pair_rows, pair_seq, pair_seq, pair_seq, gates],
        out_specs=pair_rows,
        out_shape=jax.ShapeDtypeStruct((b, s, w), BF16),
        scratch_shapes=_attend_scratch(FOX_PAIRS),
        compiler_params=_params("parallel", "parallel", "arbitrary"),
        name="fox_attention",
    )(q_lo, q_hi, k, v_lo, v_hi, c_t)


def _swa_body(sink_ref, q_ref, k_ref, vlo_ref, vhi_ref, bias_ref, o_ref):
    w = WINDOW
    n_sub = SWA_TQ // w
    n_groups = A_HEADS // 2
    v_refs = (vlo_ref, vhi_ref)
    for sb in range(n_sub):
        blk = pl.program_id(1) * n_sub + sb
        k_start = pl.multiple_of(jnp.maximum(blk - 1, 0) * w, w)
        table = jnp.minimum(blk, 1)
        k_tile = k_ref[0, pl.ds(k_start, 2 * w), :]
        heads = [(grp, half) for grp in range(n_groups) for half in range(2)]
        scores = []
        for grp, half in heads:
            q_half = _split_pair(q_ref[0, sb * w:(sb + 1) * w, grp * LANES:(grp + 1) * LANES])[half]
            scores.append(_dot_nt(q_half, k_tile) + bias_ref[table, grp + n_groups * half])
        probs = []
        for (grp, half), s in zip(heads, scores):
            chunks = _lane_chunks(s)
            sink = sink_ref[grp + n_groups * half]
            row_max = jnp.max(functools.reduce(jnp.maximum, chunks), axis=-1, keepdims=True)
            m = jnp.maximum(jnp.broadcast_to(row_max, (w, LANES)), sink)
            e = jnp.concatenate([jnp.exp2(c - m).astype(BF16) for c in chunks], axis=1)
            probs.append((e, jnp.exp2(sink - m)))
        out = []
        for (grp, half), (e, sink_term) in zip(heads, probs):
            pv = _dot(e, v_refs[half][0, pl.ds(k_start, 2 * w), :])
            out.append(pv / (pltpu.roll(pv, HALF, axis=1) + sink_term))
        for grp in range(n_groups):
            o_ref[0, sb * w:(sb + 1) * w, grp * LANES:(grp + 1) * LANES] = jnp.where(
                _lane_halves(out[0].shape), out[2 * grp], out[2 * grp + 1]).astype(BF16)


def _swa_bias_tables():
    w = WINDOW
    row = lax.broadcasted_iota(jnp.int32, (w, 2 * w), 0)
    col = lax.broadcasted_iota(jnp.int32, (w, 2 * w), 1)
    slopes = 2.0 ** (-8.0 * jnp.arange(1, A_HEADS + 1, dtype=F32) / A_HEADS)
    tables = []
    for dist in (row - col, row + w - col):
        band = (dist >= 0) & (dist < w)
        alibi = -LOG2E * slopes[:, None, None] * dist.astype(F32)[None]
        tables.append(jnp.where(band[None], alibi, -jnp.inf))
    return jnp.stack(tables)


def _swa_attention(q, k, v_lo, v_hi, sinks):
    b, s, wq = q.shape
    tq = SWA_TQ
    bias = _swa_bias_tables()
    rows = pl.BlockSpec((1, tq, wq), lambda bi, qi: (bi, qi, 0))
    seq = pl.BlockSpec((1, s, LANES), lambda bi, qi: (bi, 0, 0))
    return pl.pallas_call(
        _swa_body,
        grid=(b, s // tq),
        in_specs=[pl.BlockSpec(memory_space=pltpu.SMEM), rows, seq, seq, seq, _resident(bias.shape)],
        out_specs=rows,
        out_shape=jax.ShapeDtypeStruct((b, s, wq), BF16),
        compiler_params=_params("parallel", "arbitrary"),
        name="swa_attention",
    )(sinks, q, k, v_lo, v_hi, bias)


def _swa_head_perm():
    n_groups = A_HEADS // 2
    idx = []
    for grp in range(n_groups):
        for head in (grp, grp + n_groups):
            idx.extend(range(head * A_HEAD_DIM, (head + 1) * A_HEAD_DIM))
    return np.asarray(idx, np.int32)


def _pad_cols(w, width):
    return jnp.pad(w, ((0, 0), (0, width - w.shape[1])))


def _even_weights(w_in, w_uq, w_ukv, w_out):
    perm = _swa_head_perm()
    nq = A_HEADS * A_HEAD_DIM
    nkv = A_KV_HEADS * A_HEAD_DIM
    o_cq = nq + 2 * nkv
    o_ckv = o_cq + B_Q_LORA
    o_kr = o_ckv + B_KV_LORA
    aq = w_in[:, :nq][:, perm] * (A_HEAD_DIM ** -0.5)
    kr = w_in[:, o_kr:o_kr + B_ROPE_DIM]
    kr_placed = jnp.pad(kr, ((0, 0), (B_NOPE_DIM, LANES - B_NOPE_DIM - B_ROPE_DIM)))
    win = jnp.concatenate([aq, w_in[:, nq:o_kr], kr_placed], axis=1).astype(BF16)

    half = B_ROPE_DIM // 2
    uq = w_uq.reshape(B_Q_LORA, B_HEADS, B_NOPE_DIM + B_ROPE_DIM)
    uq_swap = jnp.concatenate([jnp.zeros_like(uq[..., :B_NOPE_DIM]),
                               uq[..., B_NOPE_DIM + half:], uq[..., B_NOPE_DIM:B_NOPE_DIM + half]], axis=-1)
    pad_head = lambda x: jnp.pad(x, ((0, 0), (0, 0), (0, LANES - x.shape[-1]))).reshape(x.shape[0], B_HEADS * LANES)
    wuq = pad_head(uq).astype(BF16)
    wuqs = pad_head(uq_swap).astype(BF16)

    ukv = w_ukv.reshape(B_KV_LORA, B_HEADS, B_NOPE_DIM + B_V_DIM)
    wk = pad_head(ukv[..., :B_NOPE_DIM]).astype(BF16)
    wv = ukv[..., B_NOPE_DIM:].reshape(B_KV_LORA, B_HEADS * B_V_DIM).astype(BF16)

    wo_a = w_out[:nq][perm].astype(BF16)
    wo_b = w_out[nq:].astype(BF16)
    return win, wuq, wuqs, wk, wv, wo_a, wo_b


def _rope_tables(seq):
    inv = ROPE_THETA ** (-jnp.arange(0, B_ROPE_DIM, 2, dtype=F32) / B_ROPE_DIM)
    ang = jnp.arange(seq, dtype=F32)[:, None] * inv[None, :]
    cos, sin = jnp.cos(ang), jnp.sin(ang)
    ones = jnp.ones((seq, B_NOPE_DIM), F32)
    zeros = jnp.zeros((seq, B_NOPE_DIM), F32)
    tail = jnp.zeros((seq, LANES - B_NOPE_DIM - B_ROPE_DIM), F32)
    c_tab = jnp.concatenate([ones, cos, cos, tail], axis=1)
    s_tab = jnp.concatenate([zeros, -sin, sin, tail], axis=1)
    scale = (B_NOPE_DIM + B_ROPE_DIM) ** -0.5 * LOG2E
    return c_tab * scale, s_tab * scale, c_tab, s_tab


def _odd_weights(w_in, b_f):
    w = C_HEADS * C_HEAD_DIM
    win = jnp.concatenate([w_in[:, :3 * w], _pad_cols(w_in[:, 3 * w:], LANES)], axis=1).astype(BF16)
    bf = _pad_cols(b_f.reshape(1, C_HEADS).astype(F32), LANES)
    return win, bf


def kernel(x, p, ffa_norm, ffa_w_gate_up, ffa_w_down, mix_norm, ffb_norm, ffb_w_gate_up, ffb_w_down, ple_norm, ple_w_gate, ple_w_proj, ev_w_in, ev_sinks, ev_cq_norm, ev_w_uq, ev_ckv_norm, ev_w_ukv, ev_w_out, od_w_in, od_b_f, od_w_out, final_norm):
    batch, seq, d = x.shape
    depth = p.shape[0]
    t = batch * seq
    h = x.reshape(t, d)
    p_all = p.reshape(depth, t, PLE_DIM)
    vec = lambda a: a.reshape(1, -1).astype(F32)
    vecs = lambda a: a.reshape(a.shape[0], 1, a.shape[1]).astype(F32)
    tabs = _rope_tables(seq)
    ga, gm, gb, gp = vecs(ffa_norm), vecs(mix_norm), vecs(ffb_norm), vecs(ple_norm)
    wa_gu, wa_d = ffa_w_gate_up.astype(BF16), ffa_w_down.astype(BF16)
    wb_gu, wb_d = ffb_w_gate_up.astype(BF16), ffb_w_down.astype(BF16)
    wg, wp = ple_w_gate.astype(BF16), ple_w_proj.astype(BF16)
    gf = vec(final_norm)

    for i in range(depth):
        j = i // 2
        if i % 2 == 0:
            win, wuq, wuqs, wk, wv, wo_a, wo_b = _even_weights(ev_w_in[j], ev_w_uq[j], ev_w_ukv[j], ev_w_out[j])
            small = (win, vec(ev_cq_norm[j]), wuq, wuqs, vec(ev_ckv_norm[j]), wk, wv)
            h, aq, ak, av_lo, av_hi, mq, mk, mv_lo, mv_hi = _pre_even(h, i, ga, wa_gu, wa_d, gm, small, tabs, seq)
            b3 = lambda a: a.reshape(batch, seq, a.shape[-1])
            out_a = _swa_attention(b3(aq), b3(ak), b3(av_lo), b3(av_hi), ev_sinks[j].astype(F32) * LOG2E)
            out_b = _mla_attention(b3(mq), b3(mk), b3(mv_lo), b3(mv_hi))
            xs, wos = [out_a.reshape(t, -1), out_b.reshape(t, -1)], [wo_a, wo_b]
        else:
            win, bf = _odd_weights(od_w_in[j], od_b_f[j])
            h3, q_lo, q_hi, k, v_lo, v_hi, c_t = _pre_odd(h.reshape(batch, seq, d), i, ga, wa_gu, wa_d, gm, win, bf)
            h = h3.reshape(t, d)
            out = _fox_attention(q_lo, q_hi, k, v_lo, v_hi, c_t.reshape(batch, C_HEADS // 2, 2, seq))
            xs, wos = [out.reshape(t, -1)], [od_w_out[j].astype(BF16)]
        h = _post(h, xs, wos, i, gb, wb_gu, wb_d, p_all, gp, wg, wp, gf, final=(i == depth - 1))
    return h.reshape(batch, seq, d)
```
